```python
import jax, jax.numpy as jnp
from jax import lax
import numpy as np

D_MODEL = 1024
BATCH = 32
SEQ = 2048
DEPTH = 1

D_MIX = 2 * D_MODEL
GM_WIDTH = D_MIX // 2
GM_HEADS = 8
GM_HEAD_DIM = GM_WIDTH // GM_HEADS
GM_CHUNK = 128
SSD_WIDTH = D_MIX - GM_WIDTH
SSD_HEAD_DIM = 64
SSD_HEADS = SSD_WIDTH // SSD_HEAD_DIM
SSD_GROUPS = 2
SSD_HPG = SSD_HEADS // SSD_GROUPS
SSD_STATE = 128
SSD_CONV = 4
SSD_CHUNK = 128
SSD_CONV_DIM = SSD_WIDTH + 2 * SSD_GROUPS * SSD_STATE
D_IN_PROJ = 2 * GM_WIDTH + SSD_WIDTH + SSD_CONV_DIM + SSD_HEADS
PEER_HEADS = 8
PEER_NKEYS = 128
PEER_EXPERTS = PEER_NKEYS * PEER_NKEYS
PEER_DKEY = 256
PEER_TOPK = 16
PEER_TOKEN_BLOCK = 128
EPS = 1e-6

kernel_name = "hybrid_gmlp_ssd_peer_block"


def rmsnorm(x, g):
    xf = x.astype(jnp.float32)
    y = xf * lax.rsqrt(jnp.mean(xf * xf, axis=-1, keepdims=True) + EPS)
    return (y * g.astype(jnp.float32)).astype(x.dtype)


def layernorm(x, g, b):
    xf = x.astype(jnp.float32)
    mu = jnp.mean(xf, axis=-1, keepdims=True)
    var = jnp.mean(jnp.square(xf - mu), axis=-1, keepdims=True)
    y = (xf - mu) * lax.rsqrt(var + EPS)
    return (y * g.astype(jnp.float32) + b.astype(jnp.float32)).astype(x.dtype)


def chunked_spatial_gating(u_raw, v_raw, ln_g, ln_b, ws, bs):
    b_, L, _ = u_raw.shape
    nc = L // GM_CHUNK
    u = jax.nn.gelu(u_raw, approximate=False)
    v = layernorm(jax.nn.gelu(v_raw, approximate=False), ln_g, ln_b)
    v = v.reshape(b_, nc, GM_CHUNK, GM_HEADS, GM_HEAD_DIM)
    causal = jnp.tril(jnp.ones((GM_CHUNK, GM_CHUNK), dtype=bool))
    w = jnp.where(causal[None], ws, jnp.zeros_like(ws))
    s = jnp.einsum('hts,bcshd->bcthd', w, v) + bs.T[None, None, :, :, None]
    return u * s.reshape(b_, L, GM_WIDTH)


def causal_depthwise_conv(x, w, b):
    C = x.shape[-1]
    y = lax.conv_general_dilated(
        x, w[:, None, :].astype(x.dtype), window_strides=(1,),
        padding=[(SSD_CONV - 1, 0)],
        dimension_numbers=('NWC', 'WIO', 'NWC'),
        feature_group_count=C)
    return y + b.astype(x.dtype)


def ssd_chunked(xs, dt, A, Bm, Cm):
    b_, L = xs.shape[0], xs.shape[1]
    nc = L // SSD_CHUNK
    Q = SSD_CHUNK
    xc = xs.reshape(b_, nc, Q, SSD_GROUPS, SSD_HPG, SSD_HEAD_DIM)
    dtc = dt.reshape(b_, nc, Q, SSD_GROUPS, SSD_HPG)
    Bc = Bm.reshape(b_, nc, Q, SSD_GROUPS, SSD_STATE)
    Cc = Cm.reshape(b_, nc, Q, SSD_GROUPS, SSD_STATE)
    a_cs = jnp.cumsum(dtc * A, axis=2)
    xdt = xc * dtc[..., None]
    acs_t = jnp.moveaxis(a_cs, 2, -1)
    diff = acs_t[..., :, None] - acs_t[..., None, :]
    causal = jnp.tril(jnp.ones((Q, Q), dtype=bool))
    Lmat = jnp.exp(jnp.where(causal, diff, -jnp.inf))
    cb = jnp.einsum('bctgn,bcsgn->bcgts', Cc, Bc)
    y_diag = jnp.einsum('bcgkts,bcsgkp->bctgkp', cb[:, :, :, None] * Lmat, xdt)
    decay_states = jnp.exp(a_cs[:, :, -1:] - a_cs)
    states = jnp.einsum('bcsgn,bcsgkp->bcgkpn', Bc, xdt * decay_states[..., None])
    chunk_decay = jnp.exp(a_cs[:, :, -1])

    def step(h, inp):
        st, dec = inp
        return dec[..., None, None] * h + st, h

    h0 = jnp.zeros((b_, SSD_GROUPS, SSD_HPG, SSD_HEAD_DIM, SSD_STATE), jnp.float32)
    _, prev = lax.scan(step, h0, (jnp.moveaxis(states, 1, 0), jnp.moveaxis(chunk_decay, 1, 0)))
    prev = jnp.moveaxis(prev, 0, 1)
    y_off = jnp.einsum('bctgn,bcgkpn->bctgkp', Cc, prev) * jnp.exp(a_cs)[..., None]
    return (y_diag + y_off).reshape(b_, L, SSD_GROUPS, SSD_HPG, SSD_HEAD_DIM)


def peer(h, w_query, sub_keys1, sub_keys2, expert_u, expert_v):
    b_, L, D = h.shape
    T = b_ * L
    hf = h.reshape(T, D)
    q = (hf @ w_query).astype(jnp.float32).reshape(T, PEER_HEADS, 2, PEER_DKEY // 2)
    s1 = jnp.einsum('thd,nd->thn', q[:, :, 0], sub_keys1.astype(jnp.float32))
    s2 = jnp.einsum('thd,nd->thn', q[:, :, 1], sub_keys2.astype(jnp.float32))
    v1, i1 = lax.top_k(s1, PEER_TOPK)
    v2, i2 = lax.top_k(s2, PEER_TOPK)
    cand = (v1[..., :, None] + v2[..., None, :]).reshape(T, PEER_HEADS, PEER_TOPK * PEER_TOPK)
    cv, ci = lax.top_k(cand, PEER_TOPK)
    e1 = jnp.take_along_axis(i1, ci // PEER_TOPK, axis=-1)
    e2 = jnp.take_along_axis(i2, ci % PEER_TOPK, axis=-1)
    idx = e1 * PEER_NKEYS + e2
    gate = jax.nn.softmax(cv, axis=-1)
    nb = T // PEER_TOKEN_BLOCK
    HK = PEER_HEADS * PEER_TOPK
    hb = hf.reshape(nb, PEER_TOKEN_BLOCK, D)
    ib = idx.reshape(nb, PEER_TOKEN_BLOCK, HK)
    gb = gate.reshape(nb, PEER_TOKEN_BLOCK, HK).astype(h.dtype)

    def block(args):
        hx, ix, gx = args
        u = expert_u[ix]
        a = jax.nn.gelu(jnp.einsum('tkd,td->tk', u, hx), approximate=False) * gx
        return jnp.einsum('tk,tkd->td', a, expert_v[ix])

    out = lax.map(block, (hb, ib, gb))
    return out.reshape(b_, L, D)


def setup_inputs(seed: int = 0) -> dict:
    key = jax.random.key(seed)
    ks = jax.random.split(key, 24)
    f32 = jnp.float32
    nrm = lambda k, shape, s: jax.random.normal(k, shape, f32) * s
    gain = lambda k, shape: 1.0 + 0.1 * jax.random.normal(k, shape, f32)
    dt0 = jnp.exp(jax.random.uniform(ks[10], (DEPTH, SSD_HEADS), f32,
                                     np.float32(np.log(1e-3)), np.float32(np.log(1e-1))))
    dt_bias = dt0 + jnp.log(-jnp.expm1(-dt0))
    return {
        "x": nrm(ks[0], (BATCH, SEQ, D_MODEL), 1.0),
        "norm1_g": gain(ks[1], (DEPTH, D_MODEL)),
        "w_in": nrm(ks[2], (DEPTH, D_MODEL, D_IN_PROJ), D_MODEL ** -0.5),
        "gmlp_ln_g": gain(ks[3], (DEPTH, GM_WIDTH)),
        "gmlp_ln_b": nrm(ks[4], (DEPTH, GM_WIDTH), 0.02),
        "gmlp_ws": nrm(ks[5], (DEPTH, GM_HEADS, GM_CHUNK, GM_CHUNK), GM_CHUNK ** -0.5),
        "gmlp_bs": gain(ks[6], (DEPTH, GM_HEADS, GM_CHUNK)),
        "gmlp_out_g": gain(ks[7], (DEPTH, GM_WIDTH)),
        "conv_w": nrm(ks[8], (DEPTH, SSD_CONV, SSD_CONV_DIM), SSD_CONV ** -0.5),
        "conv_b": nrm(ks[9], (DEPTH, SSD_CONV_DIM), 0.02),
        "dt_bias": dt_bias,
        "a_log": jnp.log(jax.random.uniform(ks[11], (DEPTH, SSD_HEADS), f32, 1.0, 16.0)),
        "d_skip": gain(ks[12], (DEPTH, SSD_HEADS)),
        "ssd_norm_g": gain(ks[13], (DEPTH, SSD_WIDTH)),
        "w_out": nrm(ks[14], (DEPTH, D_MIX, D_MODEL), D_MIX ** -0.5),
        "norm2_g": gain(ks[15], (DEPTH, D_MODEL)),
        "w_query": nrm(ks[16], (DEPTH, D_MODEL, PEER_HEADS * PEER_DKEY), D_MODEL ** -0.5),
        "sub_keys1": nrm(ks[17], (DEPTH, PEER_NKEYS, PEER_DKEY // 2), (PEER_DKEY // 2) ** -0.5),
        "sub_keys2": nrm(ks[18], (DEPTH, PEER_NKEYS, PEER_DKEY // 2), (PEER_DKEY // 2) ** -0.5),
        "expert_u": nrm(ks[19], (DEPTH, PEER_EXPERTS, D_MODEL), D_MODEL ** -0.5),
        "expert_v": nrm(ks[20], (DEPTH, PEER_EXPERTS, D_MODEL), PEER_HEADS ** -0.5),
        "final_g": gain(ks[21], (D_MODEL,)),
    }


def reference(x, norm1_g, w_in, gmlp_ln_g, gmlp_ln_b, gmlp_ws, gmlp_bs, gmlp_out_g,
              conv_w, conv_b, dt_bias, a_log, d_skip, ssd_norm_g, w_out, norm2_g,
              w_query, sub_keys1, sub_keys2, expert_u, expert_v, final_g):
    b_, L, _ = x.shape
    splits = [GM_WIDTH, 2 * GM_WIDTH, 2 * GM_WIDTH + SSD_WIDTH,
              2 * GM_WIDTH + SSD_WIDTH + SSD_CONV_DIM]
    for l in range(DEPTH):
        h = rmsnorm(x, norm1_g[l])
        proj = h @ w_in[l]
        u_raw, v_raw, z, xbc, dt_raw = jnp.split(proj, splits, axis=-1)
        gm = chunked_spatial_gating(u_raw, v_raw, gmlp_ln_g[l], gmlp_ln_b[l],
                                    gmlp_ws[l], gmlp_bs[l])
        gm = rmsnorm(gm, gmlp_out_g[l])
        xbc = jax.nn.silu(causal_depthwise_conv(xbc, conv_w[l], conv_b[l])).astype(jnp.float32)
        xs, Bm, Cm = jnp.split(xbc, [SSD_WIDTH, SSD_WIDTH + SSD_GROUPS * SSD_STATE], axis=-1)
        xs = xs.reshape(b_, L, SSD_GROUPS, SSD_HPG, SSD_HEAD_DIM)
        Bm = Bm.reshape(b_, L, SSD_GROUPS, SSD_STATE)
        Cm = Cm.reshape(b_, L, SSD_GROUPS, SSD_STATE)
        dt = jax.nn.softplus(dt_raw.astype(jnp.float32) + dt_bias[l].astype(jnp.float32))
        dt = dt.reshape(b_, L, SSD_GROUPS, SSD_HPG)
        A = -jnp.exp(a_log[l].astype(jnp.float32)).reshape(SSD_GROUPS, SSD_HPG)
        y = ssd_chunked(xs, dt, A, Bm, Cm)
        y = y + d_skip[l].astype(jnp.float32).reshape(SSD_GROUPS, SSD_HPG)[..., None] * xs
        yg = (y.reshape(b_, L, SSD_WIDTH) * jax.nn.silu(z.astype(jnp.float32)))
        yg = yg.reshape(b_, L, SSD_GROUPS, SSD_WIDTH // SSD_GROUPS)
        yg = yg * lax.rsqrt(jnp.mean(yg * yg, axis=-1, keepdims=True) + EPS)
        ssd_out = (yg.reshape(b_, L, SSD_WIDTH) * ssd_norm_g[l].astype(jnp.float32)).astype(x.dtype)
        mix = jnp.concatenate([gm.astype(x.dtype), ssd_out], axis=-1) @ w_out[l]
        x = x + mix.astype(x.dtype)
        h2 = rmsnorm(x, norm2_g[l])
        x = x + peer(h2, w_query[l], sub_keys1[l], sub_keys2[l],
                     expert_u[l], expert_v[l]).astype(x.dtype)
    return rmsnorm(x, final_g)
```

```python
import functools
import math

import jax
import jax.numpy as jnp
from jax import lax
from jax.experimental import pallas as pl
from jax.experimental.pallas import tpu as pltpu

F32 = jnp.float32
BF16 = jnp.bfloat16
EPS = 1e-6

LANES = 128
VMEM_LIMIT_BYTES = 56 * 1024 * 1024

GM_HEADS = 8
SSD_HEADS = 16
SSD_HEAD_DIM = 64
SSD_GROUPS = 2
SSD_CONV = 4
PEER_HEADS = 8
PEER_TOPK = 16

MIX_TILE = 512
PREP_TILE = 512
DENSE_TILE = 512
DENSE_EXPERT_BLOCK = 1024


def _dot(a, b):
    return jnp.dot(a, b, preferred_element_type=F32)


def _dot_nt(a, b):
    return lax.dot_general(a, b, (((1,), (1,)), ((), ())), preferred_element_type=F32)


def _split3(v):
    hi = v.astype(BF16)
    r = v - hi.astype(F32)
    mid = r.astype(BF16)
    lo = (r - mid.astype(F32)).astype(BF16)
    return hi, mid, lo


def _dot_f32_rhs(a_bf16, b):
    hi, mid, lo = _split3(b)
    return _dot(a_bf16, hi) + _dot(a_bf16, mid) + _dot(a_bf16, lo)


def _dot_f32_lhs(a, b_bf16):
    hi, mid, lo = _split3(a)
    return _dot(hi, b_bf16) + _dot(mid, b_bf16) + _dot(lo, b_bf16)


def _gelu(x):
    return 0.5 * x * (1.0 + lax.erf(x * math.sqrt(0.5)))


def _silu(x):
    return x * jax.nn.sigmoid(x)


def _rms(x, g):
    return x * lax.rsqrt(jnp.mean(x * x, axis=-1, keepdims=True) + EPS) * g


def _mixer_kernel(x_ref, g1_ref, wu_ref, wv_ref, wz_ref, wxbc_ref, wdt_ref,
                  lng_ref, lnb_ref, ws_ref, bst_ref, gog_ref,
                  convw_ref, convb_ref, dtb_ref, alog_ref, dskip_ref, sng_ref,
                  expand_ref, wout_ref,
                  o_ref,
                  st_ref, ext_ref, xbc_ref, dt_ref, da_ref, gm_ref, y_ref, mix_ref):
    tm = x_ref.shape[0]
    nchunk = tm // LANES
    gm_w = wu_ref.shape[1]
    ssd_w = wz_ref.shape[1]
    grp_w = ssd_w // SSD_GROUPS

    @pl.when(pl.program_id(1) == 0)
    def _():
        st_ref[...] = jnp.zeros_like(st_ref)
        ext_ref[0:8, :] = jnp.zeros((8, ext_ref.shape[1]), F32)

    x = x_ref[...]
    h = _rms(x, g1_ref[...]).astype(BF16)

    row = lax.broadcasted_iota(jnp.int32, (LANES, LANES), 0)
    col = lax.broadcasted_iota(jnp.int32, (LANES, LANES), 1)
    causal = row >= col

    u = _gelu(_dot(h, wu_ref[...]))
    vg = _gelu(_dot(h, wv_ref[...]))
    mu = jnp.mean(vg, axis=-1, keepdims=True)
    var = jnp.mean(jnp.square(vg - mu), axis=-1, keepdims=True)
    v = ((vg - mu) * lax.rsqrt(var + EPS) * lng_ref[...] + lnb_ref[...]).astype(BF16)
    for hd in range(GM_HEADS):
        wm = jnp.where(causal, ws_ref[hd], 0.0).astype(BF16)
        bcol = bst_ref[:, hd:hd + 1]
        for c in range(nchunk):
            rs = slice(c * LANES, (c + 1) * LANES)
            cs = slice(hd * LANES, (hd + 1) * LANES)
            gm_ref[rs, cs] = u[rs, cs] * (_dot(wm, v[rs, cs]) + bcol)
    mix_ref[:, 0:gm_w] = _rms(gm_ref[...], gog_ref[...]).astype(BF16)

    ext_ref[8:8 + tm, :] = _dot(h, wxbc_ref[...])
    conv = convb_ref[...]
    for k in range(SSD_CONV):
        conv = conv + ext_ref[8 - (SSD_CONV - 1) + k:8 - (SSD_CONV - 1) + k + tm, :] * convw_ref[k:k + 1, :]
    ext_ref[0:8, :] = ext_ref[tm:tm + 8, :]
    xbc_ref[...] = _silu(conv)

    dt = jax.nn.softplus(_dot(h, wdt_ref[...]) + dtb_ref[...])
    dt_ref[...] = dt
    da_ref[...] = dt * (-jnp.exp(alog_ref[...]))

    tri = causal.astype(BF16)
    lane = lax.broadcasted_iota(jnp.int32, (LANES, LANES), 1)
    low_half = lane < SSD_HEAD_DIM
    expand = expand_ref[...]
    dskip = dskip_ref[...]
    b_off = ssd_w
    c_off = ssd_w + SSD_GROUPS * LANES

    def chunk(c, carry):
        r0 = pl.multiple_of(c * LANES, LANES)
        rows = pl.ds(r0, LANES)
        acs = _dot_f32_rhs(tri, da_ref[rows, :])
        acs_t = acs.T
        last = acs[LANES - 1:LANES, :]
        dtx = _dot_f32_lhs(dt_ref[rows, :], expand)
        dsx = _dot_f32_lhs(jnp.exp(last - acs), expand)
        eax = _dot_f32_lhs(jnp.exp(acs), expand)
        cdx = _dot_f32_lhs(jnp.broadcast_to(jnp.exp(last), (8, LANES)), expand)[0:1, :]
        xs = xbc_ref[rows, 0:ssd_w]
        xdt = xs * dtx
        xw = xdt * dsx
        for g in range(SSD_GROUPS):
            gcols = slice(g * grp_w, (g + 1) * grp_w)
            bg = xbc_ref[rows, b_off + g * LANES:b_off + (g + 1) * LANES]
            cg = xbc_ref[rows, c_off + g * LANES:c_off + (g + 1) * LANES].astype(BF16)
            bt = bg.T.astype(BF16)
            cb = _dot(cg, bt)
            st = st_ref[:, gcols]
            yoff = _dot(cg, st.astype(BF16)) * eax[:, gcols]
            st_ref[:, gcols] = st * cdx[:, gcols] + _dot(bt, xw[:, gcols].astype(BF16))
            hpg = SSD_HEADS // SSD_GROUPS
            for kp in range(hpg // 2):
                h0 = g * hpg + 2 * kp
                pcols = slice(h0 * SSD_HEAD_DIM, h0 * SSD_HEAD_DIM + LANES)
                ms = []
                for hh in (h0, h0 + 1):
                    diff = acs[:, hh:hh + 1] - acs_t[hh:hh + 1, :]
                    ms.append(cb * jnp.exp(jnp.where(causal, diff, -jnp.inf)))
                lhs = jnp.concatenate(ms, axis=1).astype(BF16)
                xp = xdt[:, pcols]
                rhs = jnp.concatenate([jnp.where(low_half, xp, 0.0),
                                       jnp.where(low_half, 0.0, xp)], axis=0).astype(BF16)
                yd = _dot(lhs, rhs)
                y_ref[rows, pcols] = (yd + yoff[:, kp * LANES:(kp + 1) * LANES]
                                      + dskip[:, pcols] * xs[:, pcols])
        return carry

    lax.fori_loop(0, nchunk, chunk, 0)

    z = _dot(h, wz_ref[...])
    yg = y_ref[...] * _silu(z)
    for g in range(SSD_GROUPS):
        gcols = slice(g * grp_w, (g + 1) * grp_w)
        mix_ref[:, gm_w + g * grp_w:gm_w + (g + 1) * grp_w] = _rms(yg[:, gcols], sng_ref[:, gcols]).astype(BF16)

    o_ref[...] = x + _dot(mix_ref[...], wout_ref[...])


def _mixer(x, p):
    b, l, d = x.shape
    tm = min(MIX_TILE, l)
    conv_dim = p["wxbc"].shape[1]
    ssd_w = p["wz"].shape[1]
    gm_w = p["wu"].shape[1]
    full = lambda a: pl.BlockSpec(a.shape, lambda i, j: (0,) * a.ndim)
    names = ["g1", "wu", "wv", "wz", "wxbc", "wdt", "lng", "lnb", "ws", "bst", "gog",
             "convw", "convb", "dtb", "alog", "dskip", "sng", "expand", "wout"]
    args = [p[n] for n in names]
    return pl.pallas_call(
        _mixer_kernel,
        grid=(b, l // tm),
        in_specs=[pl.BlockSpec((None, tm, d), lambda i, j: (i, j, 0))] + [full(a) for a in args],
        out_specs=pl.BlockSpec((None, tm, d), lambda i, j: (i, j, 0)),
        out_shape=jax.ShapeDtypeStruct((b, l, d), F32),
        scratch_shapes=[
            pltpu.VMEM((LANES, ssd_w), F32),
            pltpu.VMEM((tm + 8, conv_dim), F32),
            pltpu.VMEM((tm, conv_dim), F32),
            pltpu.VMEM((tm, LANES), F32),
            pltpu.VMEM((tm, LANES), F32),
            pltpu.VMEM((tm, gm_w), F32),
            pltpu.VMEM((tm, ssd_w), F32),
            pltpu.VMEM((tm, gm_w + ssd_w), BF16),
        ],
        compiler_params=pltpu.CompilerParams(
            dimension_semantics=("arbitrary", "arbitrary"),
            vmem_limit_bytes=VMEM_LIMIT_BYTES),
        name="mixer",
    )(x, *args)


def _top16(s):
    rowi = lax.broadcasted_iota(jnp.int32, (PEER_TOPK, s.shape[1]), 0)
    out = jnp.zeros((PEER_TOPK, s.shape[1]), F32)
    for r in range(PEER_TOPK):
        m = jnp.max(s, axis=0, keepdims=True)
        out = jnp.where(rowi == r, m, out)
        s = jnp.where(s == m, -jnp.inf, s)
    return out


def _candidates(a16, b16):
    parts = [a16 * b16[0:1]]
    for j in range(1, 8):
        parts.append(a16[0:8] * b16[j:j + 1])
    parts.append(a16[0:1] * b16[8:16])
    return jnp.concatenate(parts, axis=0)


def _prep_kernel(x_ref, g2_ref, wq_ref, k1_ref, k2_ref,
                 ht_ref, w_ref, e2_ref, th_ref,
                 q_ref):
    tt = x_ref.shape[0]
    h2 = _rms(x_ref[...], g2_ref[...])
    hb = h2.astype(BF16)
    ht_ref[...] = hb.astype(F32).T.astype(BF16)
    q_ref[...] = _dot(hb, wq_ref[...])
    k1 = k1_ref[...]
    k2 = k2_ref[...]
    dk = k1.shape[1]

    def group(lg, carry):
        rows = pl.ds(pl.multiple_of(lg * LANES, LANES), LANES)
        for hd in range(PEER_HEADS):
            q1 = q_ref[rows, hd * 2 * dk:hd * 2 * dk + dk].astype(BF16)
            q2 = q_ref[rows, hd * 2 * dk + dk:(hd + 1) * 2 * dk].astype(BF16)
            s1 = _dot_nt(k1, q1)
            s2 = _dot_nt(k2, q2)
            v1 = _top16(s1)
            v2 = _top16(s2)
            m1 = v1[0:1]
            m2 = v2[0:1]
            e1 = jnp.exp(s1 - m1)
            e2 = jnp.exp(s2 - m2)
            e1s = jnp.exp(v1 - m1)
            e2s = jnp.exp(v2 - m2)
            cand = _candidates(e1s, e2s)
            cw = cand
            zsum = jnp.zeros((1, LANES), F32)
            pm = zsum
            for _ in range(PEER_TOPK):
                pm = jnp.max(cw, axis=0, keepdims=True)
                zsum = zsum + pm
                cw = jnp.where(cw == pm, -1.0, cw)
            rz = 1.0 / zsum
            scaled = _candidates(e1s * rz, e2s)
            th = jnp.max(jnp.where(cand == pm, scaled, 0.0), axis=0, keepdims=True)
            w_ref[lg, hd] = e1 * rz
            e2_ref[lg, hd] = e2
            th_ref[lg, hd:hd + 1, :] = th
        return carry

    lax.fori_loop(0, tt // LANES, group, 0)


def _peer_prep(x1, g2, wq, k1, k2):
    t, d = x1.shape
    tt = min(PREP_TILE, t)
    ng = tt // LANES
    nk = k1.shape[0]
    full = lambda a: pl.BlockSpec(a.shape, lambda i: (0,) * a.ndim)
    return pl.pallas_call(
        _prep_kernel,
        grid=(t // tt,),
        in_specs=[pl.BlockSpec((tt, d), lambda i: (i, 0)), full(g2), full(wq), full(k1), full(k2)],
        out_specs=[
            pl.BlockSpec((d, tt), lambda i: (0, i)),
            pl.BlockSpec((ng, PEER_HEADS, nk, LANES), lambda i: (i, 0, 0, 0)),
            pl.BlockSpec((ng, PEER_HEADS, nk, LANES), lambda i: (i, 0, 0, 0)),
            pl.BlockSpec((ng, PEER_HEADS, LANES), lambda i: (i, 0, 0)),
        ],
        out_shape=[
            jax.ShapeDtypeStruct((d, t), BF16),
            jax.ShapeDtypeStruct((t // LANES, PEER_HEADS, nk, LANES), F32),
            jax.ShapeDtypeStruct((t // LANES, PEER_HEADS, nk, LANES), F32),
            jax.ShapeDtypeStruct((t // LANES, PEER_HEADS, LANES), F32),
        ],
        scratch_shapes=[pltpu.VMEM((tt, wq.shape[1]), F32)],
        compiler_params=pltpu.CompilerParams(
            dimension_semantics=("arbitrary",),
            vmem_limit_bytes=VMEM_LIMIT_BYTES),
        name="peer_prep",
    )(x1, g2, wq, k1, k2)


def _dense_kernel(ht_ref, u_ref, vt_ref, w_ref, e2_ref, th_ref, x_ref, fg_ref,
                  o_ref,
                  acc_ref, a_ref, b_ref, *, apply_final_norm):
    e = pl.program_id(1)
    ne = pl.num_programs(1)
    eb = u_ref.shape[0]
    tt = ht_ref.shape[1]
    ib = eb // LANES

    @pl.when(e == 0)
    def _():
        acc_ref[...] = jnp.zeros_like(acc_ref)

    a_ref[...] = _dot(u_ref[...], ht_ref[...])

    for lg in range(tt // LANES):
        lanes = slice(lg * LANES, (lg + 1) * LANES)
        ths = [th_ref[lg, hd:hd + 1, :] for hd in range(PEER_HEADS)]

        def body(ii, carry):
            i = e * ib + ii
            rows = pl.ds(pl.multiple_of(ii * LANES, LANES), LANES)
            gate = jnp.zeros((LANES, LANES), F32)
            for hd in range(PEER_HEADS):
                val = e2_ref[lg, hd] * w_ref[lg, hd, pl.ds(i, 1), :]
                gate = gate + jnp.where(val >= ths[hd], val, 0.0)
            b_ref[rows, lanes] = (_gelu(a_ref[rows, lanes]) * gate).astype(BF16)
            return carry

        lax.fori_loop(0, ib, body, 0)

    acc_ref[...] += _dot(vt_ref[...], b_ref[...])

    @pl.when(e == ne - 1)
    def _():
        y = x_ref[...] + acc_ref[...].T
        if apply_final_norm:
            y = _rms(y, fg_ref[...])
        o_ref[...] = y


def _peer_dense(ht, u, vt, w, e2, th, x1, fg, apply_final_norm):
    d, t = ht.shape
    ne = u.shape[0]
    tt = min(DENSE_TILE, t)
    eb = min(DENSE_EXPERT_BLOCK, ne)
    ng = tt // LANES
    nk = w.shape[2]
    return pl.pallas_call(
        functools.partial(_dense_kernel, apply_final_norm=apply_final_norm),
        grid=(t // tt, ne // eb),
        in_specs=[
            pl.BlockSpec((d, tt), lambda i, j: (0, i)),
            pl.BlockSpec((eb, d), lambda i, j: (j, 0)),
            pl.BlockSpec((d, eb), lambda i, j: (0, j)),
            pl.BlockSpec((ng, PEER_HEADS, nk, LANES), lambda i, j: (i, 0, 0, 0)),
            pl.BlockSpec((ng, PEER_HEADS, nk, LANES), lambda i, j: (i, 0, 0, 0)),
            pl.BlockSpec((ng, PEER_HEADS, LANES), lambda i, j: (i, 0, 0)),
            pl.BlockSpec((tt, d), lambda i, j: (i, 0)),
            pl.BlockSpec(fg.shape, lambda i, j: (0, 0)),
        ],
        out_specs=pl.BlockSpec((tt, d), lambda i, j: (i, 0)),
        out_shape=jax.ShapeDtypeStruct((t, d), F32),
        scratch_shapes=[
            pltpu.VMEM((d, tt), F32),
            pltpu.VMEM((eb, tt), F32),
            pltpu.VMEM((eb, tt), BF16),
        ],
        compiler_params=pltpu.CompilerParams(
            dimension_semantics=("arbitrary", "arbitrary"),
            vmem_limit_bytes=VMEM_LIMIT_BYTES),
        name="peer_dense",
    )(ht, u, vt, w, e2, th, x1, fg)


def _pad_lanes(a, width=LANES):
    return jnp.pad(a, [(0, 0)] * (a.ndim - 1) + [(0, width - a.shape[-1])])


def kernel(x, norm1_g, w_in, gmlp_ln_g, gmlp_ln_b, gmlp_ws, gmlp_bs, gmlp_out_g, conv_w, conv_b, dt_bias, a_log, d_skip, ssd_norm_g, w_out, norm2_g, w_query, sub_keys1, sub_keys2, expert_u, expert_v, final_g):
    b, l, d = x.shape
    depth = w_in.shape[0]
    gm_w = gmlp_ln_g.shape[1]
    ssd_w = ssd_norm_g.shape[1]
    conv_dim = conv_w.shape[2]
    assert ssd_w == SSD_HEADS * SSD_HEAD_DIM and gmlp_ws.shape[1:] == (GM_HEADS, LANES, LANES)
    assert l % LANES == 0 and (b * l) % LANES == 0
    expand = (jnp.arange(LANES)[:, None] == (jnp.arange(ssd_w)[None, :] // SSD_HEAD_DIM)).astype(BF16)
    row2 = lambda a: a.reshape(1, -1).astype(F32)
    for layer in range(depth):
        wl = w_in[layer].astype(BF16)
        o1, o2, o3, o4 = gm_w, 2 * gm_w, 2 * gm_w + ssd_w, 2 * gm_w + ssd_w + conv_dim
        p = dict(
            g1=row2(norm1_g[layer]),
            wu=wl[:, :o1], wv=wl[:, o1:o2], wz=wl[:, o2:o3], wxbc=wl[:, o3:o4],
            wdt=_pad_lanes(wl[:, o4:]),
            lng=row2(gmlp_ln_g[layer]), lnb=row2(gmlp_ln_b[layer]),
            ws=gmlp_ws[layer], bst=_pad_lanes(gmlp_bs[layer].T),
            gog=row2(gmlp_out_g[layer]),
            convw=conv_w[layer], convb=row2(conv_b[layer]),
            dtb=_pad_lanes(row2(dt_bias[layer])), alog=_pad_lanes(row2(a_log[layer])),
            dskip=row2(jnp.repeat(d_skip[layer], SSD_HEAD_DIM)),
            sng=row2(ssd_norm_g[layer]),
            expand=expand, wout=w_out[layer].astype(BF16),
        )
        x1 = _mixer(x, p).reshape(b * l, d)
        ht, w, e2, th = _peer_prep(x1, row2(norm2_g[layer]), w_query[layer].astype(BF16),
                                   sub_keys1[layer].astype(BF16), sub_keys2[layer].astype(BF16))
        last = layer == depth - 1
        x = _peer_dense(ht, expert_u[layer].astype(BF16), expert_v[layer].astype(BF16).T,
                        w, e2, th, x1, row2(final_g), apply_final_norm=last).reshape(b, l, d)
    return x
```

```python
import functools
import math

import jax
import jax.numpy as jnp
from jax import lax
from jax.experimental import pallas as pl
from jax.experimental.pallas import tpu as pltpu

F32 = jnp.float32
BF16 = jnp.bfloat16
EPS = 1e-6

LANES = 128
VMEM_LIMIT_BYTES = 56 * 1024 * 1024

GM_HEADS = 8
SSD_HEADS = 16
SSD_HEAD_DIM = 64
SSD_GROUPS = 2
SSD_CONV = 4
PEER_HEADS = 8
PEER_TOPK = 16

MIX_TILE = 512
PREP_TILE = 512
DENSE_TILE = 512
DENSE_EXPERT_BLOCK = 1024


def _dot(a, b):
    return jnp.dot(a, b, preferred_element_type=F32)


def _dot_nt(a, b):
    return lax.dot_general(a, b, (((1,), (1,)), ((), ())), preferred_element_type=F32)


def _split3(v):
    hi = v.astype(BF16)
    r = v - hi.astype(F32)
    mid = r.astype(BF16)
    lo = (r - mid.astype(F32)).astype(BF16)
    return hi, mid, lo


def _dot_f32_rhs(a_bf16, b):
    hi, mid, lo = _split3(b)
    return _dot(a_bf16, hi) + _dot(a_bf16, mid) + _dot(a_bf16, lo)


def _dot_f32_lhs(a, b_bf16):
    hi, mid, lo = _split3(a)
    return _dot(hi, b_bf16) + _dot(mid, b_bf16) + _dot(lo, b_bf16)


def _gelu(x):
    return 0.5 * x * (1.0 + lax.erf(x * math.sqrt(0.5)))


def _silu(x):
    return x * jax.nn.sigmoid(x)


def _rms(x, g):
    return x * lax.rsqrt(jnp.mean(x * x, axis=-1, keepdims=True) + EPS) * g


def _mixer_kernel(x_ref, g1_ref, wu_ref, wv_ref, wz_ref, wxbc_ref, wdt_ref,
                  lng_ref, lnb_ref, ws_ref, bst_ref, gog_ref,
                  convw_ref, convb_ref, dtb_ref, alog_ref, dskip_ref, sng_ref,
                  expand_ref, wout_ref,
                  o_ref,
                  st_ref, ext_ref, xbc_ref, dt_ref, da_ref, gm_ref, y_ref, mix_ref):
    tm = x_ref.shape[0]
    nchunk = tm // LANES
    gm_w = wu_ref.shape[1]
    ssd_w = wz_ref.shape[1]
    grp_w = ssd_w // SSD_GROUPS

    @pl.when(pl.program_id(1) == 0)
    def _():
        st_ref[...] = jnp.zeros_like(st_ref)
        ext_ref[0:8, :] = jnp.zeros((8, ext_ref.shape[1]), F32)

    x = x_ref[...]
    h = _rms(x, g1_ref[...]).astype(BF16)

    row = lax.broadcasted_iota(jnp.int32, (LANES, LANES), 0)
    col = lax.broadcasted_iota(jnp.int32, (LANES, LANES), 1)
    causal = row >= col

    u = _gelu(_dot(h, wu_ref[...]))
    vg = _gelu(_dot(h, wv_ref[...]))
    mu = jnp.mean(vg, axis=-1, keepdims=True)
    var = jnp.mean(jnp.square(vg - mu), axis=-1, keepdims=True)
    v = ((vg - mu) * lax.rsqrt(var + EPS) * lng_ref[...] + lnb_ref[...]).astype(BF16)
    for hd in range(GM_HEADS):
        wm = jnp.where(causal, ws_ref[hd], 0.0).astype(BF16)
        bcol = bst_ref[:, hd:hd + 1]
        for c in range(nchunk):
            rs = slice(c * LANES, (c + 1) * LANES)
            cs = slice(hd * LANES, (hd + 1) * LANES)
            gm_ref[rs, cs] = u[rs, cs] * (_dot(wm, v[rs, cs]) + bcol)
    mix_ref[:, 0:gm_w] = _rms(gm_ref[...], gog_ref[...]).astype(BF16)

    ext_ref[8:8 + tm, :] = _dot(h, wxbc_ref[...])
    conv = convb_ref[...]
    for k in range(SSD_CONV):
        conv = conv + ext_ref[8 - (SSD_CONV - 1) + k:8 - (SSD_CONV - 1) + k + tm, :] * convw_ref[k:k + 1, :]
    ext_ref[0:8, :] = ext_ref[tm:tm + 8, :]
    xbc_ref[...] = _silu(conv)

    dt = jax.nn.softplus(_dot(h, wdt_ref[...]) + dtb_ref[...])
    dt_ref[...] = dt
    da_ref[...] = dt * (-jnp.exp(alog_ref[...]))

    tri = causal.astype(BF16)
    lane = lax.broadcasted_iota(jnp.int32, (LANES, LANES), 1)
    low_half = lane < SSD_HEAD_DIM
    expand = expand_ref[...]
    dskip = dskip_ref[...]
    b_off = ssd_w
    c_off = ssd_w + SSD_GROUPS * LANES

    def chunk(c, carry):
        r0 = pl.multiple_of(c * LANES, LANES)
        rows = pl.ds(r0, LANES)
        acs = _dot_f32_rhs(tri, da_ref[rows, :])
        acs_t = acs.T
        last = acs[LANES - 1:LANES, :]
        dtx = _dot_f32_lhs(dt_ref[rows, :], expand)
        dsx = _dot_f32_lhs(jnp.exp(last - acs), expand)
        eax = _dot_f32_lhs(jnp.exp(acs), expand)
        cdx = _dot_f32_lhs(jnp.broadcast_to(jnp.exp(last), (8, LANES)), expand)[0:1, :]
        xs = xbc_ref[rows, 0:ssd_w]
        xdt = xs * dtx
        xw = xdt * dsx
        for g in range(SSD_GROUPS):
            gcols = slice(g * grp_w, (g + 1) * grp_w)
            bg = xbc_ref[rows, b_off + g * LANES:b_off + (g + 1) * LANES]
            cg = xbc_ref[rows, c_off + g * LANES:c_off + (g + 1) * LANES].astype(BF16)
            bt = bg.T.astype(BF16)
            cb = _dot(cg, bt)
            st = st_ref[:, gcols]
            yoff = _dot(cg, st.astype(BF16)) * eax[:, gcols]
            st_ref[:, gcols] = st * cdx[:, gcols] + _dot(bt, xw[:, gcols].astype(BF16))
            hpg = SSD_HEADS // SSD_GROUPS
            for kp in range(hpg // 2):
                h0 = g * hpg + 2 * kp
                pcols = slice(h0 * SSD_HEAD_DIM, h0 * SSD_HEAD_DIM + LANES)
                ms = []
                for hh in (h0, h0 + 1):
                    diff = acs[:, hh:hh + 1] - acs_t[hh:hh + 1, :]
                    ms.append(cb * jnp.exp(jnp.where(causal, diff, -jnp.inf)))
                lhs = jnp.concatenate(ms, axis=1).astype(BF16)
                xp = xdt[:, pcols]
                rhs = jnp.concatenate([jnp.where(low_half, xp, 0.0),
                                       jnp.where(low_half, 0.0, xp)], axis=0).astype(BF16)
                yd = _dot(lhs, rhs)
                y_ref[rows, pcols] = (yd + yoff[:, kp * LANES:(kp + 1) * LANES]
                                      + dskip[:, pcols] * xs[:, pcols])
        return carry

    lax.fori_loop(0, nchunk, chunk, 0)

    z = _dot(h, wz_ref[...])
    yg = y_ref[...] * _silu(z)
    for g in range(SSD_GROUPS):
        gcols = slice(g * grp_w, (g + 1) * grp_w)
        mix_ref[:, gm_w + g * grp_w:gm_w + (g + 1) * grp_w] = _rms(yg[:, gcols], sng_ref[:, gcols]).astype(BF16)

    o_ref[...] = x + _dot(mix_ref[...], wout_ref[...])


def _mixer(x, p):
    b, l, d = x.shape
    tm = min(MIX_TILE, l)
    conv_dim = p["wxbc"].shape[1]
    ssd_w = p["wz"].shape[1]
    gm_w = p["wu"].shape[1]
    full = lambda a: pl.BlockSpec(a.shape, lambda i, j: (0,) * a.ndim)
    names = ["g1", "wu", "wv", "wz", "wxbc", "wdt", "lng", "lnb", "ws", "bst", "gog",
             "convw", "convb", "dtb", "alog", "dskip", "sng", "expand", "wout"]
    args = [p[n] for n in names]
    return pl.pallas_call(
        _mixer_kernel,
        grid=(b, l // tm),
        in_specs=[pl.BlockSpec((None, tm, d), lambda i, j: (i, j, 0))] + [full(a) for a in args],
        out_specs=pl.BlockSpec((None, tm, d), lambda i, j: (i, j, 0)),
        out_shape=jax.ShapeDtypeStruct((b, l, d), F32),
        scratch_shapes=[
            pltpu.VMEM((LANES, ssd_w), F32),
            pltpu.VMEM((tm + 8, conv_dim), F32),
            pltpu.VMEM((tm, conv_dim), F32),
            pltpu.VMEM((tm, LANES), F32),
            pltpu.VMEM((tm, LANES), F32),
            pltpu.VMEM((tm, gm_w), F32),
            pltpu.VMEM((tm, ssd_w), F32),
            pltpu.VMEM((tm, gm_w + ssd_w), BF16),
        ],
        compiler_params=pltpu.CompilerParams(
            dimension_semantics=("arbitrary", "arbitrary"),
            vmem_limit_bytes=VMEM_LIMIT_BYTES),
        name="mixer",
    )(x, *args)


def _top16(s):
    rowi = lax.broadcasted_iota(jnp.int32, (PEER_TOPK, s.shape[1]), 0)
    out = jnp.zeros((PEER_TOPK, s.shape[1]), F32)
    for r in range(PEER_TOPK):
        m = jnp.max(s, axis=0, keepdims=True)
        out = jnp.where(rowi == r, m, out)
        s = jnp.where(s == m, -jnp.inf, s)
    return out


def _candidates(a16, b16):
    parts = [a16 * b16[0:1]]
    for j in range(1, 8):
        parts.append(a16[0:8] * b16[j:j + 1])
    parts.append(a16[0:1] * b16[8:16])
    return jnp.concatenate(parts, axis=0)


def _prep_kernel(x_ref, g2_ref, wq_ref, k1_ref, k2_ref,
                 ht_ref, w_ref, e2_ref, th_ref,
                 q_ref):
    tt = x_ref.shape[0]
    h2 = _rms(x_ref[...], g2_ref[...])
    hb = h2.astype(BF16)
    ht_ref[...] = hb.astype(F32).T.astype(BF16)
    q_ref[...] = _dot(hb, wq_ref[...])
    k1 = k1_ref[...]
    k2 = k2_ref[...]
    dk = k1.shape[1]

    def group(lg, carry):
        rows = pl.ds(pl.multiple_of(lg * LANES, LANES), LANES)
        for hd in range(PEER_HEADS):
            q1 = q_ref[rows, hd * 2 * dk:hd * 2 * dk + dk].astype(BF16)
            q2 = q_ref[rows, hd * 2 * dk + dk:(hd + 1) * 2 * dk].astype(BF16)
            s1 = _dot_nt(k1, q1)
            s2 = _dot_nt(k2, q2)
            v1 = _top16(s1)
            v2 = _top16(s2)
            m1 = v1[0:1]
            m2 = v2[0:1]
            e1 = jnp.exp(s1 - m1)
            e2 = jnp.exp(s2 - m2)
            e1s = jnp.exp(v1 - m1)
            e2s = jnp.exp(v2 - m2)
            cand = _candidates(e1s, e2s)
            cw = cand
            zsum = jnp.zeros((1, LANES), F32)
            pm = zsum
            for _ in range(PEER_TOPK):
                pm = jnp.max(cw, axis=0, keepdims=True)
                zsum = zsum + pm
                cw = jnp.where(cw == pm, -1.0, cw)
            rz = 1.0 / zsum
            scaled = _candidates(e1s * rz, e2s)
            th = jnp.max(jnp.where(cand == pm, scaled, 0.0), axis=0, keepdims=True)
            w_ref[lg, hd] = e1 * rz
            e2_ref[lg, hd] = e2
            th_ref[lg, hd:hd + 1, :] = th
        return carry

    lax.fori_loop(0, tt // LANES, group, 0)


def _peer_prep(x1, g2, wq, k1, k2):
    t, d = x1.shape
    tt = min(PREP_TILE, t)
    ng = tt // LANES
    nk = k1.shape[0]
    full = lambda a: pl.BlockSpec(a.shape, lambda i: (0,) * a.ndim)
    return pl.pallas_call(
        _prep_kernel,
        grid=(t // tt,),
        in_specs=[pl.BlockSpec((tt, d), lambda i: (i, 0)), full(g2), full(wq), full(k1), full(k2)],
        out_specs=[
            pl.BlockSpec((d, tt), lambda i: (0, i)),
            pl.BlockSpec((ng, PEER_HEADS, nk, LANES), lambda i: (i, 0, 0, 0)),
            pl.BlockSpec((ng, PEER_HEADS, nk, LANES), lambda i: (i, 0, 0, 0)),
            pl.BlockSpec((ng, PEER_HEADS, LANES), lambda i: (i, 0, 0)),
        ],
        out_shape=[
            jax.ShapeDtypeStruct((d, t), BF16),
            jax.ShapeDtypeStruct((t // LANES, PEER_HEADS, nk, LANES), F32),
            jax.ShapeDtypeStruct((t // LANES, PEER_HEADS, nk, LANES), F32),
            jax.ShapeDtypeStruct((t // LANES, PEER_HEADS, LANES), F32),
        ],
        scratch_shapes=[pltpu.VMEM((tt, wq.shape[1]), F32)],
        compiler_params=pltpu.CompilerParams(
            dimension_semantics=("arbitrary",),
            vmem_limit_bytes=VMEM_LIMIT_BYTES),
        name="peer_prep",
    )(x1, g2, wq, k1, k2)


def _dense_kernel(ht_ref, u_ref, vt_ref, w_ref, e2_ref, th_ref, x_ref, fg_ref,
                  o_ref,
                  acc_ref, a0_ref, a1_ref, b0_ref, b1_ref, *, apply_final_norm, npairs):
    g = pl.program_id(1)
    eb = u_ref.shape[0]
    tt = ht_ref.shape[1]
    ib = eb // LANES
    hb = 2 * vt_ref.shape[0] // ib
    nblocks = 2 * npairs

    @pl.when(g == 0)
    def _():
        acc_ref[...] = jnp.zeros_like(acc_ref)
        a0_ref[...] = jnp.zeros_like(a0_ref)
        a1_ref[...] = jnp.zeros_like(a1_ref)
        b0_ref[...] = jnp.zeros_like(b0_ref)
        b1_ref[...] = jnp.zeros_like(b1_ref)

    def phase(half, a_new, a_old, b_new, b_old):
        i0 = jnp.clip(2 * g + half - 1, 0, nblocks - 1) * ib

        def sub(ii, carry):
            rows = pl.ds(pl.multiple_of(ii * LANES, LANES), LANES)
            urow = pl.ds(pl.multiple_of((half * eb + ii * LANES) // 2, LANES // 2), LANES // 2)
            a_new[rows, :] = _dot(pltpu.bitcast(u_ref[urow, :], BF16), ht_ref[...])
            for lg in range(tt // LANES):
                lanes = slice(lg * LANES, (lg + 1) * LANES)
                gate = jnp.zeros((LANES, LANES), F32)
                for hd in range(PEER_HEADS):
                    val = e2_ref[lg, hd] * w_ref[lg, hd, pl.ds(i0 + ii, 1), :]
                    gate = gate + jnp.where(val >= th_ref[lg, hd:hd + 1, :], val, 0.0)
                b_new[rows, lanes] = (_gelu(a_old[rows, lanes]) * gate).astype(BF16)
            orow = pl.ds(pl.multiple_of(ii * hb, hb), hb)
            vrow = pl.ds(pl.multiple_of(ii * (hb // 2), hb // 2), hb // 2)
            acc_ref[orow, :] += _dot(pltpu.bitcast(vt_ref[vrow, half * eb:(half + 1) * eb], BF16), b_old[...])
            return carry

        lax.fori_loop(0, ib, sub, 0, unroll=4)

    phase(0, a0_ref, a1_ref, b1_ref, b0_ref)
    phase(1, a1_ref, a0_ref, b0_ref, b1_ref)

    @pl.when(g == npairs)
    def _():
        y = x_ref[...] + acc_ref[...].T
        if apply_final_norm:
            y = _rms(y, fg_ref[...])
        o_ref[...] = y


def _peer_dense(ht, u, vt, w, e2, th, x1, fg, apply_final_norm):
    d, t = ht.shape
    ne = 2 * u.shape[0]
    tt = min(DENSE_TILE, t)
    eb = min(DENSE_EXPERT_BLOCK, ne // 2)
    npairs = ne // (2 * eb)
    ng = tt // LANES
    nk = w.shape[2]
    return pl.pallas_call(
        functools.partial(_dense_kernel, apply_final_norm=apply_final_norm, npairs=npairs),
        grid=(t // tt, npairs + 1),
        in_specs=[
            pl.BlockSpec((d, tt), lambda i, g: (0, i)),
            pl.BlockSpec((eb, d), lambda i, g: (jnp.minimum(g, npairs - 1), 0)),
            pl.BlockSpec((d // 2, 2 * eb), lambda i, g: (0, jnp.maximum(g - 1, 0))),
            pl.BlockSpec((ng, PEER_HEADS, nk, LANES), lambda i, g: (i, 0, 0, 0)),
            pl.BlockSpec((ng, PEER_HEADS, nk, LANES), lambda i, g: (i, 0, 0, 0)),
            pl.BlockSpec((ng, PEER_HEADS, LANES), lambda i, g: (i, 0, 0)),
            pl.BlockSpec((tt, d), lambda i, g: (i, 0)),
            pl.BlockSpec(fg.shape, lambda i, g: (0, 0)),
        ],
        out_specs=pl.BlockSpec((tt, d), lambda i, g: (i, 0)),
        out_shape=jax.ShapeDtypeStruct((t, d), F32),
        scratch_shapes=[
            pltpu.VMEM((d, tt), F32),
            pltpu.VMEM((eb, tt), F32),
            pltpu.VMEM((eb, tt), F32),
            pltpu.VMEM((eb, tt), BF16),
            pltpu.VMEM((eb, tt), BF16),
        ],
        compiler_params=pltpu.CompilerParams(
            dimension_semantics=("arbitrary", "arbitrary"),
            vmem_limit_bytes=VMEM_LIMIT_BYTES),
        name="peer_dense",
    )(ht, u, vt, w, e2, th, x1, fg)


def _pack_rows(a):
    h = lax.bitcast_convert_type(a, jnp.uint16).astype(jnp.uint32).reshape(a.shape[0] // 2, 2, a.shape[1])
    return h[:, 0] | (h[:, 1] << 16)


def _pad_lanes(a, width=LANES):
    return jnp.pad(a, [(0, 0)] * (a.ndim - 1) + [(0, width - a.shape[-1])])


def kernel(x, norm1_g, w_in, gmlp_ln_g, gmlp_ln_b, gmlp_ws, gmlp_bs, gmlp_out_g, conv_w, conv_b, dt_bias, a_log, d_skip, ssd_norm_g, w_out, norm2_g, w_query, sub_keys1, sub_keys2, expert_u, expert_v, final_g):
    b, l, d = x.shape
    depth = w_in.shape[0]
    gm_w = gmlp_ln_g.shape[1]
    ssd_w = ssd_norm_g.shape[1]
    conv_dim = conv_w.shape[2]
    assert ssd_w == SSD_HEADS * SSD_HEAD_DIM and gmlp_ws.shape[1:] == (GM_HEADS, LANES, LANES)
    assert l % LANES == 0 and (b * l) % LANES == 0
    expand = (jnp.arange(LANES)[:, None] == (jnp.arange(ssd_w)[None, :] // SSD_HEAD_DIM)).astype(BF16)
    row2 = lambda a: a.reshape(1, -1).astype(F32)
    for layer in range(depth):
        wl = w_in[layer].astype(BF16)
        o1, o2, o3, o4 = gm_w, 2 * gm_w, 2 * gm_w + ssd_w, 2 * gm_w + ssd_w + conv_dim
        p = dict(
            g1=row2(norm1_g[layer]),
            wu=wl[:, :o1], wv=wl[:, o1:o2], wz=wl[:, o2:o3], wxbc=wl[:, o3:o4],
            wdt=_pad_lanes(wl[:, o4:]),
            lng=row2(gmlp_ln_g[layer]), lnb=row2(gmlp_ln_b[layer]),
            ws=gmlp_ws[layer], bst=_pad_lanes(gmlp_bs[layer].T),
            gog=row2(gmlp_out_g[layer]),
            convw=conv_w[layer], convb=row2(conv_b[layer]),
            dtb=_pad_lanes(row2(dt_bias[layer])), alog=_pad_lanes(row2(a_log[layer])),
            dskip=row2(jnp.repeat(d_skip[layer], SSD_HEAD_DIM)),
            sng=row2(ssd_norm_g[layer]),
            expand=expand, wout=w_out[layer].astype(BF16),
        )
        x1 = _mixer(x, p).reshape(b * l, d)
        ht, w, e2, th = _peer_prep(x1, row2(norm2_g[layer]), w_query[layer].astype(BF16),
                                   sub_keys1[layer].astype(BF16), sub_keys2[layer].astype(BF16))
        last = layer == depth - 1
        x = _peer_dense(ht, _pack_rows(expert_u[layer].astype(BF16)), _pack_rows(expert_v[layer].astype(BF16).T),
                        w, e2, th, x1, row2(final_g), apply_final_norm=last).reshape(b, l, d)
    return x
```

```python
import functools
import math

import jax
import jax.numpy as jnp
from jax import lax
from jax.experimental import pallas as pl
from jax.experimental.pallas import tpu as pltpu

F32 = jnp.float32
BF16 = jnp.bfloat16
EPS = 1e-6

LANES = 128
VMEM_LIMIT_BYTES = 56 * 1024 * 1024

GM_HEADS = 8
SSD_HEADS = 16
SSD_HEAD_DIM = 64
SSD_GROUPS = 2
SSD_CONV = 4
PEER_HEADS = 8
PEER_TOPK = 16

MIX_TILE = 512
PREP_TILE = 512
DENSE_TILE = 512
DENSE_EXPERT_BLOCK = 1024
DENSE_SUB = 2
DENSE_UNROLL = 2


def _dot(a, b):
    return jnp.dot(a, b, preferred_element_type=F32)


def _dot_nt(a, b):
    return lax.dot_general(a, b, (((1,), (1,)), ((), ())), preferred_element_type=F32)


def _split3(v):
    hi = v.astype(BF16)
    r = v - hi.astype(F32)
    mid = r.astype(BF16)
    lo = (r - mid.astype(F32)).astype(BF16)
    return hi, mid, lo


def _dot_f32_rhs(a_bf16, b):
    hi, mid, lo = _split3(b)
    return _dot(a_bf16, hi) + _dot(a_bf16, mid) + _dot(a_bf16, lo)


def _dot_f32_lhs(a, b_bf16):
    hi, mid, lo = _split3(a)
    return _dot(hi, b_bf16) + _dot(mid, b_bf16) + _dot(lo, b_bf16)


def _gelu(x):
    return 0.5 * x * (1.0 + lax.erf(x * math.sqrt(0.5)))


def _silu(x):
    return x * jax.nn.sigmoid(x)


def _rms(x, g):
    return x * lax.rsqrt(jnp.mean(x * x, axis=-1, keepdims=True) + EPS) * g


def _mixer_kernel(x_ref, g1_ref, wu_ref, wv_ref, wz_ref, wxbc_ref, wdt_ref,
                  lng_ref, lnb_ref, ws_ref, bst_ref, gog_ref,
                  convw_ref, convb_ref, dtb_ref, alog_ref, dskip_ref, sng_ref,
                  expand_ref, wout_ref,
                  o_ref,
                  st_ref, ext_ref, xbc_ref, dt_ref, da_ref, gm_ref, y_ref, mix_ref):
    tm = x_ref.shape[0]
    nchunk = tm // LANES
    gm_w = wu_ref.shape[1]
    ssd_w = wz_ref.shape[1]
    grp_w = ssd_w // SSD_GROUPS

    @pl.when(pl.program_id(1) == 0)
    def _():
        st_ref[...] = jnp.zeros_like(st_ref)
        ext_ref[0:8, :] = jnp.zeros((8, ext_ref.shape[1]), F32)

    x = x_ref[...]
    h = _rms(x, g1_ref[...]).astype(BF16)

    row = lax.broadcasted_iota(jnp.int32, (LANES, LANES), 0)
    col = lax.broadcasted_iota(jnp.int32, (LANES, LANES), 1)
    causal = row >= col

    u = _gelu(_dot(h, wu_ref[...]))
    vg = _gelu(_dot(h, wv_ref[...]))
    mu = jnp.mean(vg, axis=-1, keepdims=True)
    var = jnp.mean(jnp.square(vg - mu), axis=-1, keepdims=True)
    v = ((vg - mu) * lax.rsqrt(var + EPS) * lng_ref[...] + lnb_ref[...]).astype(BF16)
    for hd in range(GM_HEADS):
        wm = jnp.where(causal, ws_ref[hd], 0.0).astype(BF16)
        bcol = bst_ref[:, hd:hd + 1]
        for c in range(nchunk):
            rs = slice(c * LANES, (c + 1) * LANES)
            cs = slice(hd * LANES, (hd + 1) * LANES)
            gm_ref[rs, cs] = u[rs, cs] * (_dot(wm, v[rs, cs]) + bcol)
    mix_ref[:, 0:gm_w] = _rms(gm_ref[...], gog_ref[...]).astype(BF16)

    ext_ref[8:8 + tm, :] = _dot(h, wxbc_ref[...])
    conv = convb_ref[...]
    for k in range(SSD_CONV):
        conv = conv + ext_ref[8 - (SSD_CONV - 1) + k:8 - (SSD_CONV - 1) + k + tm, :] * convw_ref[k:k + 1, :]
    ext_ref[0:8, :] = ext_ref[tm:tm + 8, :]
    xbc_ref[...] = _silu(conv)

    dt = jax.nn.softplus(_dot(h, wdt_ref[...]) + dtb_ref[...])
    dt_ref[...] = dt
    da_ref[...] = dt * (-jnp.exp(alog_ref[...]))

    tri = causal.astype(BF16)
    lane = lax.broadcasted_iota(jnp.int32, (LANES, LANES), 1)
    low_half = lane < SSD_HEAD_DIM
    expand = expand_ref[...]
    dskip = dskip_ref[...]
    b_off = ssd_w
    c_off = ssd_w + SSD_GROUPS * LANES

    def chunk(c, carry):
        r0 = pl.multiple_of(c * LANES, LANES)
        rows = pl.ds(r0, LANES)
        acs = _dot_f32_rhs(tri, da_ref[rows, :])
        acs_t = acs.T
        last = acs[LANES - 1:LANES, :]
        dtx = _dot_f32_lhs(dt_ref[rows, :], expand)
        dsx = _dot_f32_lhs(jnp.exp(last - acs), expand)
        eax = _dot_f32_lhs(jnp.exp(acs), expand)
        cdx = _dot_f32_lhs(jnp.broadcast_to(jnp.exp(last), (8, LANES)), expand)[0:1, :]
        xs = xbc_ref[rows, 0:ssd_w]
        xdt = xs * dtx
        xw = xdt * dsx
        for g in range(SSD_GROUPS):
            gcols = slice(g * grp_w, (g + 1) * grp_w)
            bg = xbc_ref[rows, b_off + g * LANES:b_off + (g + 1) * LANES]
            cg = xbc_ref[rows, c_off + g * LANES:c_off + (g + 1) * LANES].astype(BF16)
            bt = bg.T.astype(BF16)
            cb = _dot(cg, bt)
            st = st_ref[:, gcols]
            yoff = _dot(cg, st.astype(BF16)) * eax[:, gcols]
            st_ref[:, gcols] = st * cdx[:, gcols] + _dot(bt, xw[:, gcols].astype(BF16))
            hpg = SSD_HEADS // SSD_GROUPS
            for kp in range(hpg // 2):
                h0 = g * hpg + 2 * kp
                pcols = slice(h0 * SSD_HEAD_DIM, h0 * SSD_HEAD_DIM + LANES)
                ms = []
                for hh in (h0, h0 + 1):
                    diff = acs[:, hh:hh + 1] - acs_t[hh:hh + 1, :]
                    ms.append(cb * jnp.exp(jnp.where(causal, diff, -jnp.inf)))
                lhs = jnp.concatenate(ms, axis=1).astype(BF16)
                xp = xdt[:, pcols]
                rhs = jnp.concatenate([jnp.where(low_half, xp, 0.0),
                                       jnp.where(low_half, 0.0, xp)], axis=0).astype(BF16)
                yd = _dot(lhs, rhs)
                y_ref[rows, pcols] = (yd + yoff[:, kp * LANES:(kp + 1) * LANES]
                                      + dskip[:, pcols] * xs[:, pcols])
        return carry

    lax.fori_loop(0, nchunk, chunk, 0)

    z = _dot(h, wz_ref[...])
    yg = y_ref[...] * _silu(z)
    for g in range(SSD_GROUPS):
        gcols = slice(g * grp_w, (g + 1) * grp_w)
        mix_ref[:, gm_w + g * grp_w:gm_w + (g + 1) * grp_w] = _rms(yg[:, gcols], sng_ref[:, gcols]).astype(BF16)

    o_ref[...] = x + _dot(mix_ref[...], wout_ref[...])


def _mixer(x, p):
    b, l, d = x.shape
    tm = min(MIX_TILE, l)
    conv_dim = p["wxbc"].shape[1]
    ssd_w = p["wz"].shape[1]
    gm_w = p["wu"].shape[1]
    full = lambda a: pl.BlockSpec(a.shape, lambda i, j: (0,) * a.ndim)
    names = ["g1", "wu", "wv", "wz", "wxbc", "wdt", "lng", "lnb", "ws", "bst", "gog",
             "convw", "convb", "dtb", "alog", "dskip", "sng", "expand", "wout"]
    args = [p[n] for n in names]
    return pl.pallas_call(
        _mixer_kernel,
        grid=(b, l // tm),
        in_specs=[pl.BlockSpec((None, tm, d), lambda i, j: (i, j, 0))] + [full(a) for a in args],
        out_specs=pl.BlockSpec((None, tm, d), lambda i, j: (i, j, 0)),
        out_shape=jax.ShapeDtypeStruct((b, l, d), F32),
        scratch_shapes=[
            pltpu.VMEM((LANES, ssd_w), F32),
            pltpu.VMEM((tm + 8, conv_dim), F32),
            pltpu.VMEM((tm, conv_dim), F32),
            pltpu.VMEM((tm, LANES), F32),
            pltpu.VMEM((tm, LANES), F32),
            pltpu.VMEM((tm, gm_w), F32),
            pltpu.VMEM((tm, ssd_w), F32),
            pltpu.VMEM((tm, gm_w + ssd_w), BF16),
        ],
        compiler_params=pltpu.CompilerParams(
            dimension_semantics=("arbitrary", "arbitrary"),
            vmem_limit_bytes=VMEM_LIMIT_BYTES),
        name="mixer",
    )(x, *args)


def _top16(s):
    rowi = lax.broadcasted_iota(jnp.int32, (PEER_TOPK, s.shape[1]), 0)
    out = jnp.zeros((PEER_TOPK, s.shape[1]), F32)
    rank = jnp.full(s.shape, float(PEER_TOPK), F32)
    for r in range(PEER_TOPK):
        m = jnp.max(s, axis=0, keepdims=True)
        out = jnp.where(rowi == r, m, out)
        hit = s == m
        rank = jnp.where(hit, float(r), rank)
        s = jnp.where(hit, -jnp.inf, s)
    return out, rank


def _candidates(a16, b16):
    parts = [a16 * b16[0:1]]
    for j in range(1, 8):
        parts.append(a16[0:8] * b16[j:j + 1])
    parts.append(a16[0:1] * b16[8:16])
    return jnp.concatenate(parts, axis=0)


def _prep_kernel(x_ref, g2_ref, wq_ref, k1_ref, k2_ref,
                 ht_ref, w_ref, n_ref, e2_ref, r2_ref,
                 q_ref):
    tt = x_ref.shape[0]
    h2 = _rms(x_ref[...], g2_ref[...])
    hb = h2.astype(BF16)
    ht_ref[...] = hb.astype(F32).T.astype(BF16)
    q_ref[...] = _dot(hb, wq_ref[...])
    k1 = k1_ref[...]
    k2 = k2_ref[...]
    dk = k1.shape[1]

    def group(lg, carry):
        rows = pl.ds(pl.multiple_of(lg * LANES, LANES), LANES)
        for hd in range(PEER_HEADS):
            q1 = q_ref[rows, hd * 2 * dk:hd * 2 * dk + dk].astype(BF16)
            q2 = q_ref[rows, hd * 2 * dk + dk:(hd + 1) * 2 * dk].astype(BF16)
            s1 = _dot_nt(k1, q1)
            s2 = _dot_nt(k2, q2)
            v1, _ = _top16(s1)
            v2, rank2 = _top16(s2)
            m1 = v1[0:1]
            m2 = v2[0:1]
            e1 = jnp.exp(s1 - m1)
            e2 = jnp.exp(s2 - m2)
            e1s = jnp.exp(v1 - m1)
            e2s = jnp.exp(v2 - m2)
            cand = _candidates(e1s, e2s)
            cw = cand
            zsum = jnp.zeros((1, LANES), F32)
            pm = zsum
            for _ in range(PEER_TOPK):
                pm = jnp.max(cw, axis=0, keepdims=True)
                zsum = zsum + pm
                cw = jnp.where(cw == pm, -1.0, cw)
            rz = 1.0 / zsum
            scaled = _candidates(e1s * rz, e2s)
            th = jnp.max(jnp.where(cand == pm, scaled, 0.0), axis=0, keepdims=True)
            w = e1 * rz
            ws0 = e1s[0:1] * rz
            cnt8 = jnp.zeros((1, LANES), F32)
            for j in range(8, PEER_TOPK):
                cnt8 = cnt8 + jnp.where(ws0 * e2s[j:j + 1] >= th, 1.0, 0.0)
            npair = jnp.where(w == ws0, cnt8, 0.0)
            for j in range(8):
                npair = npair + jnp.where(w * e2s[j:j + 1] >= th, 1.0, 0.0)
            w_ref[lg, hd] = w
            n_ref[lg, hd] = npair
            e2_ref[lg, hd] = pltpu.bitcast(e2.astype(BF16), jnp.uint32)
            r2_ref[lg, hd] = pltpu.bitcast(rank2.astype(BF16), jnp.uint32)
        return carry

    lax.fori_loop(0, tt // LANES, group, 0)


def _peer_prep(x1, g2, wq, k1, k2):
    t, d = x1.shape
    tt = min(PREP_TILE, t)
    ng = tt // LANES
    nk = k1.shape[0]
    full = lambda a: pl.BlockSpec(a.shape, lambda i: (0,) * a.ndim)
    return pl.pallas_call(
        _prep_kernel,
        grid=(t // tt,),
        in_specs=[pl.BlockSpec((tt, d), lambda i: (i, 0)), full(g2), full(wq), full(k1), full(k2)],
        out_specs=[
            pl.BlockSpec((d, tt), lambda i: (0, i)),
            pl.BlockSpec((ng, PEER_HEADS, nk, LANES), lambda i: (i, 0, 0, 0)),
            pl.BlockSpec((ng, PEER_HEADS, nk, LANES), lambda i: (i, 0, 0, 0)),
            pl.BlockSpec((ng, PEER_HEADS, nk // 2, LANES), lambda i: (i, 0, 0, 0)),
            pl.BlockSpec((ng, PEER_HEADS, nk // 2, LANES), lambda i: (i, 0, 0, 0)),
        ],
        out_shape=[
            jax.ShapeDtypeStruct((d, t), BF16),
            jax.ShapeDtypeStruct((t // LANES, PEER_HEADS, nk, LANES), F32),
            jax.ShapeDtypeStruct((t // LANES, PEER_HEADS, nk, LANES), F32),
            jax.ShapeDtypeStruct((t // LANES, PEER_HEADS, nk // 2, LANES), jnp.uint32),
            jax.ShapeDtypeStruct((t // LANES, PEER_HEADS, nk // 2, LANES), jnp.uint32),
        ],
        scratch_shapes=[pltpu.VMEM((tt, wq.shape[1]), F32)],
        compiler_params=pltpu.CompilerParams(
            dimension_semantics=("arbitrary",),
            vmem_limit_bytes=VMEM_LIMIT_BYTES),
        name="peer_prep",
    )(x1, g2, wq, k1, k2)


def _dense_kernel(ht_ref, u_ref, vt_ref, w_ref, n_ref, e2_ref, r2_ref, x_ref, fg_ref,
                  o_ref,
                  acc_ref, a0_ref, a1_ref, b0_ref, b1_ref, *, apply_final_norm, npairs):
    g = pl.program_id(1)
    eb = u_ref.shape[0]
    tt = ht_ref.shape[1]
    ib = eb // LANES
    hb = 2 * vt_ref.shape[0] // ib
    sr = DENSE_SUB * LANES
    nblocks = 2 * npairs

    @pl.when(g == 0)
    def _():
        acc_ref[...] = jnp.zeros_like(acc_ref)
        a0_ref[...] = jnp.zeros_like(a0_ref)
        a1_ref[...] = jnp.zeros_like(a1_ref)
        b0_ref[...] = jnp.zeros_like(b0_ref)
        b1_ref[...] = jnp.zeros_like(b1_ref)

    def phase(half, a_new, a_old, b_new, b_old):
        i0 = jnp.clip(2 * g + half - 1, 0, nblocks - 1) * ib

        def sub(s, carry):
            r0 = pl.multiple_of(s * sr, sr)
            urow = pl.ds(pl.multiple_of((half * eb + s * sr) // 2, sr // 2), sr // 2)
            a_new[pl.ds(r0, sr), :] = _dot(pltpu.bitcast(u_ref[urow, :], BF16), ht_ref[...])
            for k in range(DENSE_SUB):
                rows = pl.ds(pl.multiple_of(r0 + k * LANES, LANES), LANES)
                for lg in range(tt // LANES):
                    lanes = slice(lg * LANES, (lg + 1) * LANES)
                    gate = jnp.zeros((LANES, LANES), BF16)
                    for hd in range(PEER_HEADS):
                        irow = pl.ds(i0 + s * DENSE_SUB + k, 1)
                        wrow = w_ref[lg, hd, irow, :].astype(BF16)
                        nrow = n_ref[lg, hd, irow, :].astype(BF16)
                        val = pltpu.bitcast(e2_ref[lg, hd], BF16) * wrow
                        gate = gate + jnp.where(pltpu.bitcast(r2_ref[lg, hd], BF16) < nrow, val, jnp.zeros_like(val))
                    b_new[rows, lanes] = _gelu(a_old[rows, lanes]).astype(BF16) * gate
            so = DENSE_SUB * hb
            orow = pl.ds(pl.multiple_of(s * so, so), so)
            vrow = pl.ds(pl.multiple_of(s * (so // 2), so // 2), so // 2)
            acc_ref[orow, :] += _dot(pltpu.bitcast(vt_ref[vrow, half * eb:(half + 1) * eb], BF16), b_old[...])
            return carry

        lax.fori_loop(0, ib // DENSE_SUB, sub, 0, unroll=DENSE_UNROLL)

    phase(0, a0_ref, a1_ref, b1_ref, b0_ref)
    phase(1, a1_ref, a0_ref, b0_ref, b1_ref)

    @pl.when(g == npairs)
    def _():
        y = x_ref[...] + acc_ref[...].T
        if apply_final_norm:
            y = _rms(y, fg_ref[...])
        o_ref[...] = y


def _peer_dense(ht, u, vt, w, n, e2, r2, x1, fg, apply_final_norm):
    d, t = ht.shape
    ne = 2 * u.shape[0]
    tt = min(DENSE_TILE, t)
    eb = min(DENSE_EXPERT_BLOCK, ne // 2)
    npairs = ne // (2 * eb)
    ng = tt // LANES
    nk = w.shape[2]
    return pl.pallas_call(
        functools.partial(_dense_kernel, apply_final_norm=apply_final_norm, npairs=npairs),
        grid=(t // tt, npairs + 1),
        in_specs=[
            pl.BlockSpec((d, tt), lambda i, g: (0, i)),
            pl.BlockSpec((eb, d), lambda i, g: (jnp.minimum(g, npairs - 1), 0)),
            pl.BlockSpec((d // 2, 2 * eb), lambda i, g: (0, jnp.maximum(g - 1, 0))),
            pl.BlockSpec((ng, PEER_HEADS, nk, LANES), lambda i, g: (i, 0, 0, 0)),
            pl.BlockSpec((ng, PEER_HEADS, nk, LANES), lambda i, g: (i, 0, 0, 0)),
            pl.BlockSpec((ng, PEER_HEADS, nk // 2, LANES), lambda i, g: (i, 0, 0, 0)),
            pl.BlockSpec((ng, PEER_HEADS, nk // 2, LANES), lambda i, g: (i, 0, 0, 0)),
            pl.BlockSpec((tt, d), lambda i, g: (i, 0)),
            pl.BlockSpec(fg.shape, lambda i, g: (0, 0)),
        ],
        out_specs=pl.BlockSpec((tt, d), lambda i, g: (i, 0)),
        out_shape=jax.ShapeDtypeStruct((t, d), F32),
        scratch_shapes=[
            pltpu.VMEM((d, tt), F32),
            pltpu.VMEM((eb, tt), F32),
            pltpu.VMEM((eb, tt), F32),
            pltpu.VMEM((eb, tt), BF16),
            pltpu.VMEM((eb, tt), BF16),
        ],
        compiler_params=pltpu.CompilerParams(
            dimension_semantics=("arbitrary", "arbitrary"),
            vmem_limit_bytes=VMEM_LIMIT_BYTES),
        name="peer_dense",
    )(ht, u, vt, w, n, e2, r2, x1, fg)


def _pack_rows(a):
    h = lax.bitcast_convert_type(a, jnp.uint16).astype(jnp.uint32).reshape(a.shape[0] // 2, 2, a.shape[1])
    return h[:, 0] | (h[:, 1] << 16)


def _pad_lanes(a, width=LANES):
    return jnp.pad(a, [(0, 0)] * (a.ndim - 1) + [(0, width - a.shape[-1])])


def kernel(x, norm1_g, w_in, gmlp_ln_g, gmlp_ln_b, gmlp_ws, gmlp_bs, gmlp_out_g, conv_w, conv_b, dt_bias, a_log, d_skip, ssd_norm_g, w_out, norm2_g, w_query, sub_keys1, sub_keys2, expert_u, expert_v, final_g):
    b, l, d = x.shape
    depth = w_in.shape[0]
    gm_w = gmlp_ln_g.shape[1]
    ssd_w = ssd_norm_g.shape[1]
    conv_dim = conv_w.shape[2]
    assert ssd_w == SSD_HEADS * SSD_HEAD_DIM and gmlp_ws.shape[1:] == (GM_HEADS, LANES, LANES)
    assert l % LANES == 0 and (b * l) % LANES == 0
    expand = (jnp.arange(LANES)[:, None] == (jnp.arange(ssd_w)[None, :] // SSD_HEAD_DIM)).astype(BF16)
    row2 = lambda a: a.reshape(1, -1).astype(F32)
    for layer in range(depth):
        wl = w_in[layer].astype(BF16)
        o1, o2, o3, o4 = gm_w, 2 * gm_w, 2 * gm_w + ssd_w, 2 * gm_w + ssd_w + conv_dim
        p = dict(
            g1=row2(norm1_g[layer]),
            wu=wl[:, :o1], wv=wl[:, o1:o2], wz=wl[:, o2:o3], wxbc=wl[:, o3:o4],
            wdt=_pad_lanes(wl[:, o4:]),
            lng=row2(gmlp_ln_g[layer]), lnb=row2(gmlp_ln_b[layer]),
            ws=gmlp_ws[layer], bst=_pad_lanes(gmlp_bs[layer].T),
            gog=row2(gmlp_out_g[layer]),
            convw=conv_w[layer], convb=row2(conv_b[layer]),
            dtb=_pad_lanes(row2(dt_bias[layer])), alog=_pad_lanes(row2(a_log[layer])),
            dskip=row2(jnp.repeat(d_skip[layer], SSD_HEAD_DIM)),
            sng=row2(ssd_norm_g[layer]),
            expand=expand, wout=w_out[layer].astype(BF16),
        )
        x1 = _mixer(x, p).reshape(b * l, d)
        ht, w, n, e2, r2 = _peer_prep(x1, row2(norm2_g[layer]), w_query[layer].astype(BF16),
                                   sub_keys1[layer].astype(BF16), sub_keys2[layer].astype(BF16))
        last = layer == depth - 1
        x = _peer_dense(ht, _pack_rows(expert_u[layer].astype(BF16)), _pack_rows(expert_v[layer].astype(BF16).T),
                        w, n, e2, r2, x1, row2(final_g), apply_final_norm=last).reshape(b, l, d)
    return x
```

```python
import functools
import math

import jax
import jax.numpy as jnp
from jax import lax
from jax.experimental import pallas as pl
from jax.experimental.pallas import tpu as pltpu

F32 = jnp.float32
BF16 = jnp.bfloat16
EPS = 1e-6

LANES = 128
VMEM_LIMIT_BYTES = 56 * 1024 * 1024

GM_HEADS = 8
SSD_HEADS = 16
SSD_HEAD_DIM = 64
SSD_GROUPS = 2
SSD_CONV = 4
PEER_HEADS = 8
PEER_TOPK = 16

MIX_TILE = 512
PREP_TILE = 512
DENSE_TILE = 1024
DENSE_EXPERT_BLOCK = 512
DENSE_SUB = 2
DENSE_UNROLL = 1


def _dot(a, b):
    return jnp.dot(a, b, preferred_element_type=F32)


def _dot_nt(a, b):
    return lax.dot_general(a, b, (((1,), (1,)), ((), ())), preferred_element_type=F32)


def _split3(v):
    hi = v.astype(BF16)
    r = v - hi.astype(F32)
    mid = r.astype(BF16)
    lo = (r - mid.astype(F32)).astype(BF16)
    return hi, mid, lo


def _dot_f32_rhs(a_bf16, b):
    hi, mid, lo = _split3(b)
    return _dot(a_bf16, hi) + _dot(a_bf16, mid) + _dot(a_bf16, lo)


def _dot_f32_lhs(a, b_bf16):
    hi, mid, lo = _split3(a)
    return _dot(hi, b_bf16) + _dot(mid, b_bf16) + _dot(lo, b_bf16)


def _gelu(x):
    return 0.5 * x * (1.0 + lax.erf(x * math.sqrt(0.5)))


def _silu(x):
    return x * jax.nn.sigmoid(x)


def _rms(x, g):
    return x * lax.rsqrt(jnp.mean(x * x, axis=-1, keepdims=True) + EPS) * g


def _mixer_kernel(x_ref, g1_ref, wu_ref, wv_ref, wz_ref, wxbc_ref, wdt_ref,
                  lng_ref, lnb_ref, ws_ref, bst_ref, gog_ref,
                  convw_ref, convb_ref, dtb_ref, alog_ref, dskip_ref, sng_ref,
                  expand_ref, wout_ref,
                  o_ref,
                  st_ref, ext_ref, xbc_ref, dt_ref, da_ref, gm_ref, y_ref, mix_ref):
    tm = x_ref.shape[0]
    nchunk = tm // LANES
    gm_w = wu_ref.shape[1]
    ssd_w = wz_ref.shape[1]
    grp_w = ssd_w // SSD_GROUPS

    @pl.when(pl.program_id(1) == 0)
    def _():
        st_ref[...] = jnp.zeros_like(st_ref)
        ext_ref[0:8, :] = jnp.zeros((8, ext_ref.shape[1]), F32)

    x = x_ref[...]
    h = _rms(x, g1_ref[...]).astype(BF16)

    row = lax.broadcasted_iota(jnp.int32, (LANES, LANES), 0)
    col = lax.broadcasted_iota(jnp.int32, (LANES, LANES), 1)
    causal = row >= col

    u = _gelu(_dot(h, wu_ref[...]))
    vg = _gelu(_dot(h, wv_ref[...]))
    mu = jnp.mean(vg, axis=-1, keepdims=True)
    var = jnp.mean(jnp.square(vg - mu), axis=-1, keepdims=True)
    v = ((vg - mu) * lax.rsqrt(var + EPS) * lng_ref[...] + lnb_ref[...]).astype(BF16)
    for hd in range(GM_HEADS):
        wm = jnp.where(causal, ws_ref[hd], 0.0).astype(BF16)
        bcol = bst_ref[:, hd:hd + 1]
        for c in range(nchunk):
            rs = slice(c * LANES, (c + 1) * LANES)
            cs = slice(hd * LANES, (hd + 1) * LANES)
            gm_ref[rs, cs] = u[rs, cs] * (_dot(wm, v[rs, cs]) + bcol)
    mix_ref[:, 0:gm_w] = _rms(gm_ref[...], gog_ref[...]).astype(BF16)

    ext_ref[8:8 + tm, :] = _dot(h, wxbc_ref[...])
    conv = convb_ref[...]
    for k in range(SSD_CONV):
        conv = conv + ext_ref[8 - (SSD_CONV - 1) + k:8 - (SSD_CONV - 1) + k + tm, :] * convw_ref[k:k + 1, :]
    ext_ref[0:8, :] = ext_ref[tm:tm + 8, :]
    xbc_ref[...] = _silu(conv)

    dt = jax.nn.softplus(_dot(h, wdt_ref[...]) + dtb_ref[...])
    dt_ref[...] = dt
    da_ref[...] = dt * (-jnp.exp(alog_ref[...]))

    tri = causal.astype(BF16)
    lane = lax.broadcasted_iota(jnp.int32, (LANES, LANES), 1)
    low_half = lane < SSD_HEAD_DIM
    expand = expand_ref[...]
    dskip = dskip_ref[...]
    b_off = ssd_w
    c_off = ssd_w + SSD_GROUPS * LANES

    def chunk(c, carry):
        r0 = pl.multiple_of(c * LANES, LANES)
        rows = pl.ds(r0, LANES)
        acs = _dot_f32_rhs(tri, da_ref[rows, :])
        acs_t = acs.T
        last = acs[LANES - 1:LANES, :]
        dtx = _dot_f32_lhs(dt_ref[rows, :], expand)
        dsx = _dot_f32_lhs(jnp.exp(last - acs), expand)
        eax = _dot_f32_lhs(jnp.exp(acs), expand)
        cdx = _dot_f32_lhs(jnp.broadcast_to(jnp.exp(last), (8, LANES)), expand)[0:1, :]
        xs = xbc_ref[rows, 0:ssd_w]
        xdt = xs * dtx
        xw = xdt * dsx
        for g in range(SSD_GROUPS):
            gcols = slice(g * grp_w, (g + 1) * grp_w)
            bg = xbc_ref[rows, b_off + g * LANES:b_off + (g + 1) * LANES]
            cg = xbc_ref[rows, c_off + g * LANES:c_off + (g + 1) * LANES].astype(BF16)
            bt = bg.T.astype(BF16)
            cb = _dot(cg, bt)
            st = st_ref[:, gcols]
            yoff = _dot(cg, st.astype(BF16)) * eax[:, gcols]
            st_ref[:, gcols] = st * cdx[:, gcols] + _dot(bt, xw[:, gcols].astype(BF16))
            hpg = SSD_HEADS // SSD_GROUPS
            for kp in range(hpg // 2):
                h0 = g * hpg + 2 * kp
                pcols = slice(h0 * SSD_HEAD_DIM, h0 * SSD_HEAD_DIM + LANES)
                ms = []
                for hh in (h0, h0 + 1):
                    diff = acs[:, hh:hh + 1] - acs_t[hh:hh + 1, :]
                    ms.append(cb * jnp.exp(jnp.where(causal, diff, -jnp.inf)))
                lhs = jnp.concatenate(ms, axis=1).astype(BF16)
                xp = xdt[:, pcols]
                rhs = jnp.concatenate([jnp.where(low_half, xp, 0.0),
                                       jnp.where(low_half, 0.0, xp)], axis=0).astype(BF16)
                yd = _dot(lhs, rhs)
                y_ref[rows, pcols] = (yd + yoff[:, kp * LANES:(kp + 1) * LANES]
                                      + dskip[:, pcols] * xs[:, pcols])
        return carry

    lax.fori_loop(0, nchunk, chunk, 0)

    z = _dot(h, wz_ref[...])
    yg = y_ref[...] * _silu(z)
    for g in range(SSD_GROUPS):
        gcols = slice(g * grp_w, (g + 1) * grp_w)
        mix_ref[:, gm_w + g * grp_w:gm_w + (g + 1) * grp_w] = _rms(yg[:, gcols], sng_ref[:, gcols]).astype(BF16)

    o_ref[...] = x + _dot(mix_ref[...], wout_ref[...])


def _mixer(x, p):
    b, l, d = x.shape
    tm = min(MIX_TILE, l)
    conv_dim = p["wxbc"].shape[1]
    ssd_w = p["wz"].shape[1]
    gm_w = p["wu"].shape[1]
    full = lambda a: pl.BlockSpec(a.shape, lambda i, j: (0,) * a.ndim)
    names = ["g1", "wu", "wv", "wz", "wxbc", "wdt", "lng", "lnb", "ws", "bst", "gog",
             "convw", "convb", "dtb", "alog", "dskip", "sng", "expand", "wout"]
    args = [p[n] for n in names]
    return pl.pallas_call(
        _mixer_kernel,
        grid=(b, l // tm),
        in_specs=[pl.BlockSpec((None, tm, d), lambda i, j: (i, j, 0))] + [full(a) for a in args],
        out_specs=pl.BlockSpec((None, tm, d), lambda i, j: (i, j, 0)),
        out_shape=jax.ShapeDtypeStruct((b, l, d), F32),
        scratch_shapes=[
            pltpu.VMEM((LANES, ssd_w), F32),
            pltpu.VMEM((tm + 8, conv_dim), F32),
            pltpu.VMEM((tm, conv_dim), F32),
            pltpu.VMEM((tm, LANES), F32),
            pltpu.VMEM((tm, LANES), F32),
            pltpu.VMEM((tm, gm_w), F32),
            pltpu.VMEM((tm, ssd_w), F32),
            pltpu.VMEM((tm, gm_w + ssd_w), BF16),
        ],
        compiler_params=pltpu.CompilerParams(
            dimension_semantics=("arbitrary", "arbitrary"),
            vmem_limit_bytes=VMEM_LIMIT_BYTES),
        name="mixer",
    )(x, *args)


def _top16(s):
    rowi = lax.broadcasted_iota(jnp.int32, (PEER_TOPK, s.shape[1]), 0)
    out = jnp.zeros((PEER_TOPK, s.shape[1]), F32)
    rank = jnp.full(s.shape, float(PEER_TOPK), F32)
    for r in range(PEER_TOPK):
        m = jnp.max(s, axis=0, keepdims=True)
        out = jnp.where(rowi == r, m, out)
        hit = s == m
        rank = jnp.where(hit, float(r), rank)
        s = jnp.where(hit, -jnp.inf, s)
    return out, rank


def _candidates(a16, b16):
    parts = [a16 * b16[0:1]]
    for j in range(1, 8):
        parts.append(a16[0:8] * b16[j:j + 1])
    parts.append(a16[0:1] * b16[8:16])
    return jnp.concatenate(parts, axis=0)


def _prep_kernel(x_ref, g2_ref, wq_ref, k1_ref, k2_ref,
                 ht_ref, w_ref, n_ref, e2_ref, r2_ref,
                 q_ref):
    tt = x_ref.shape[0]
    h2 = _rms(x_ref[...], g2_ref[...])
    hb = h2.astype(BF16)
    ht_ref[...] = pltpu.bitcast(hb.astype(F32).T.astype(BF16), jnp.uint32)
    q_ref[...] = _dot(hb, wq_ref[...])
    k1 = k1_ref[...]
    k2 = k2_ref[...]
    dk = k1.shape[1]

    def group(lg, carry):
        rows = pl.ds(pl.multiple_of(lg * LANES, LANES), LANES)
        for hd in range(PEER_HEADS):
            q1 = q_ref[rows, hd * 2 * dk:hd * 2 * dk + dk].astype(BF16)
            q2 = q_ref[rows, hd * 2 * dk + dk:(hd + 1) * 2 * dk].astype(BF16)
            s1 = _dot_nt(k1, q1)
            s2 = _dot_nt(k2, q2)
            v1, _ = _top16(s1)
            v2, rank2 = _top16(s2)
            m1 = v1[0:1]
            m2 = v2[0:1]
            e1 = jnp.exp(s1 - m1)
            e2 = jnp.exp(s2 - m2)
            e1s = jnp.exp(v1 - m1)
            e2s = jnp.exp(v2 - m2)
            cand = _candidates(e1s, e2s)
            cw = cand
            zsum = jnp.zeros((1, LANES), F32)
            pm = zsum
            for _ in range(PEER_TOPK):
                pm = jnp.max(cw, axis=0, keepdims=True)
                zsum = zsum + pm
                cw = jnp.where(cw == pm, -1.0, cw)
            rz = 1.0 / zsum
            scaled = _candidates(e1s * rz, e2s)
            th = jnp.max(jnp.where(cand == pm, scaled, 0.0), axis=0, keepdims=True)
            w = e1 * rz
            ws0 = e1s[0:1] * rz
            cnt8 = jnp.zeros((1, LANES), F32)
            for j in range(8, PEER_TOPK):
                cnt8 = cnt8 + jnp.where(ws0 * e2s[j:j + 1] >= th, 1.0, 0.0)
            npair = jnp.where(w == ws0, cnt8, 0.0)
            for j in range(8):
                npair = npair + jnp.where(w * e2s[j:j + 1] >= th, 1.0, 0.0)
            w_ref[lg, hd] = pltpu.bitcast(w.astype(BF16), jnp.uint32)
            n_ref[lg, hd] = pltpu.bitcast(npair.astype(BF16), jnp.uint32)
            e2_ref[lg, hd] = pltpu.bitcast(e2.astype(BF16), jnp.uint32)
            r2_ref[lg, hd] = pltpu.bitcast(rank2.astype(BF16), jnp.uint32)
        return carry

    lax.fori_loop(0, tt // LANES, group, 0)


def _peer_prep(x1, g2, wq, k1, k2):
    t, d = x1.shape
    tt = min(PREP_TILE, t)
    ng = tt // LANES
    nk = k1.shape[0]
    full = lambda a: pl.BlockSpec(a.shape, lambda i: (0,) * a.ndim)
    return pl.pallas_call(
        _prep_kernel,
        grid=(t // tt,),
        in_specs=[pl.BlockSpec((tt, d), lambda i: (i, 0)), full(g2), full(wq), full(k1), full(k2)],
        out_specs=[
            pl.BlockSpec((d // 2, tt), lambda i: (0, i)),
            pl.BlockSpec((ng, PEER_HEADS, nk // 2, LANES), lambda i: (i, 0, 0, 0)),
            pl.BlockSpec((ng, PEER_HEADS, nk // 2, LANES), lambda i: (i, 0, 0, 0)),
            pl.BlockSpec((ng, PEER_HEADS, nk // 2, LANES), lambda i: (i, 0, 0, 0)),
            pl.BlockSpec((ng, PEER_HEADS, nk // 2, LANES), lambda i: (i, 0, 0, 0)),
        ],
        out_shape=[
            jax.ShapeDtypeStruct((d // 2, t), jnp.uint32),
            jax.ShapeDtypeStruct((t // LANES, PEER_HEADS, nk // 2, LANES), jnp.uint32),
            jax.ShapeDtypeStruct((t // LANES, PEER_HEADS, nk // 2, LANES), jnp.uint32),
            jax.ShapeDtypeStruct((t // LANES, PEER_HEADS, nk // 2, LANES), jnp.uint32),
            jax.ShapeDtypeStruct((t // LANES, PEER_HEADS, nk // 2, LANES), jnp.uint32),
        ],
        scratch_shapes=[pltpu.VMEM((tt, wq.shape[1]), F32)],
        compiler_params=pltpu.CompilerParams(
            dimension_semantics=("arbitrary",),
            vmem_limit_bytes=VMEM_LIMIT_BYTES),
        name="peer_prep",
    )(x1, g2, wq, k1, k2)


def _half_row(words, half):
    bits = (words << 16) if half == 0 else (words & jnp.uint32(0xFFFF0000))
    return lax.bitcast_convert_type(bits, F32).astype(BF16)


def _dense_kernel(ht_ref, u_ref, vt_ref, w_ref, n_ref, e2_ref, r2_ref, x_ref, fg_ref,
                  o_ref,
                  acc_ref, a0_ref, a1_ref, b0_ref, b1_ref, *, apply_final_norm, npairs):
    g = pl.program_id(1)
    eb = u_ref.shape[0]
    tt = ht_ref.shape[1]
    ib = eb // LANES
    hb = 2 * vt_ref.shape[0] // ib
    sr = DENSE_SUB * LANES
    nblocks = 2 * npairs

    @pl.when(g == 0)
    def _():
        acc_ref[...] = jnp.zeros_like(acc_ref)
        a0_ref[...] = jnp.zeros_like(a0_ref)
        a1_ref[...] = jnp.zeros_like(a1_ref)
        b0_ref[...] = jnp.zeros_like(b0_ref)
        b1_ref[...] = jnp.zeros_like(b1_ref)

    def phase(half, a_new, a_old, b_new, b_old):
        i0 = jnp.clip(2 * g + half - 1, 0, nblocks - 1) * ib

        def sub(s, carry):
            r0 = pl.multiple_of(s * sr, sr)
            urow = pl.ds(pl.multiple_of((half * eb + s * sr) // 2, sr // 2), sr // 2)
            a_new[pl.ds(r0, sr), :] = _dot(pltpu.bitcast(u_ref[urow, :], BF16),
                                           pltpu.bitcast(ht_ref[...], BF16))
            for k in range(DENSE_SUB):
                rows = pl.ds(pl.multiple_of(r0 + k * LANES, LANES), LANES)
                for lg in range(tt // LANES):
                    lanes = slice(lg * LANES, (lg + 1) * LANES)
                    gate = jnp.zeros((LANES, LANES), BF16)
                    for hd in range(PEER_HEADS):
                        irow = pl.ds((i0 + s * DENSE_SUB) // 2 + k // 2, 1)
                        wrow = _half_row(w_ref[lg, hd, irow, :], k % 2)
                        nrow = _half_row(n_ref[lg, hd, irow, :], k % 2)
                        val = pltpu.bitcast(e2_ref[lg, hd], BF16) * wrow
                        gate = gate + jnp.where(pltpu.bitcast(r2_ref[lg, hd], BF16) < nrow, val, jnp.zeros_like(val))
                    b_new[rows, lanes] = _gelu(a_old[rows, lanes]).astype(BF16) * gate
            so = DENSE_SUB * hb
            orow = pl.ds(pl.multiple_of(s * so, so), so)
            vrow = pl.ds(pl.multiple_of(s * (so // 2), so // 2), so // 2)
            acc_ref[orow, :] += _dot(pltpu.bitcast(vt_ref[vrow, half * eb:(half + 1) * eb], BF16), b_old[...])
            return carry

        lax.fori_loop(0, ib // DENSE_SUB, sub, 0, unroll=DENSE_UNROLL)

    phase(0, a0_ref, a1_ref, b1_ref, b0_ref)
    phase(1, a1_ref, a0_ref, b0_ref, b1_ref)

    @pl.when(g == npairs)
    def _():
        y = x_ref[...] + acc_ref[...].T
        if apply_final_norm:
            y = _rms(y, fg_ref[...])
        o_ref[...] = y


def _peer_dense(ht, u, vt, w, n, e2, r2, x1, fg, apply_final_norm):
    d, t = 2 * ht.shape[0], ht.shape[1]
    ne = 2 * u.shape[0]
    tt = min(DENSE_TILE, t)
    eb = min(DENSE_EXPERT_BLOCK, ne // 2)
    npairs = ne // (2 * eb)
    ng = tt // LANES
    nk2 = w.shape[2]
    assert (eb // LANES) % 2 == 0 and DENSE_SUB % 2 == 0
    return pl.pallas_call(
        functools.partial(_dense_kernel, apply_final_norm=apply_final_norm, npairs=npairs),
        grid=(t // tt, npairs + 1),
        in_specs=[
            pl.BlockSpec((d // 2, tt), lambda i, g: (0, i)),
            pl.BlockSpec((eb, d), lambda i, g: (jnp.minimum(g, npairs - 1), 0)),
            pl.BlockSpec((d // 2, 2 * eb), lambda i, g: (0, jnp.maximum(g - 1, 0))),
            pl.BlockSpec((ng, PEER_HEADS, nk2, LANES), lambda i, g: (i, 0, 0, 0)),
            pl.BlockSpec((ng, PEER_HEADS, nk2, LANES), lambda i, g: (i, 0, 0, 0)),
            pl.BlockSpec((ng, PEER_HEADS, nk2, LANES), lambda i, g: (i, 0, 0, 0)),
            pl.BlockSpec((ng, PEER_HEADS, nk2, LANES), lambda i, g: (i, 0, 0, 0)),
            pl.BlockSpec((tt, d), lambda i, g: (i, 0), pipeline_mode=pl.Buffered(1)),
            pl.BlockSpec(fg.shape, lambda i, g: (0, 0)),
        ],
        out_specs=pl.BlockSpec((tt, d), lambda i, g: (i, 0), pipeline_mode=pl.Buffered(1)),
        out_shape=jax.ShapeDtypeStruct((t, d), F32),
        scratch_shapes=[
            pltpu.VMEM((d, tt), F32),
            pltpu.VMEM((eb, tt), F32),
            pltpu.VMEM((eb, tt), F32),
            pltpu.VMEM((eb, tt), BF16),
            pltpu.VMEM((eb, tt), BF16),
        ],
        compiler_params=pltpu.CompilerParams(
            dimension_semantics=("arbitrary", "arbitrary"),
            vmem_limit_bytes=VMEM_LIMIT_BYTES),
        name="peer_dense",
    )(ht, u, vt, w, n, e2, r2, x1, fg)


def _pack_rows(a):
    h = lax.bitcast_convert_type(a, jnp.uint16).astype(jnp.uint32).reshape(a.shape[0] // 2, 2, a.shape[1])
    return h[:, 0] | (h[:, 1] << 16)


def _pad_lanes(a, width=LANES):
    return jnp.pad(a, [(0, 0)] * (a.ndim - 1) + [(0, width - a.shape[-1])])


def kernel(x, norm1_g, w_in, gmlp_ln_g, gmlp_ln_b, gmlp_ws, gmlp_bs, gmlp_out_g, conv_w, conv_b, dt_bias, a_log, d_skip, ssd_norm_g, w_out, norm2_g, w_query, sub_keys1, sub_keys2, expert_u, expert_v, final_g):
    b, l, d = x.shape
    depth = w_in.shape[0]
    gm_w = gmlp_ln_g.shape[1]
    ssd_w = ssd_norm_g.shape[1]
    conv_dim = conv_w.shape[2]
    assert ssd_w == SSD_HEADS * SSD_HEAD_DIM and gmlp_ws.shape[1:] == (GM_HEADS, LANES, LANES)
    assert l % LANES == 0 and (b * l) % LANES == 0
    expand = (jnp.arange(LANES)[:, None] == (jnp.arange(ssd_w)[None, :] // SSD_HEAD_DIM)).astype(BF16)
    row2 = lambda a: a.reshape(1, -1).astype(F32)
    for layer in range(depth):
        wl = w_in[layer].astype(BF16)
        o1, o2, o3, o4 = gm_w, 2 * gm_w, 2 * gm_w + ssd_w, 2 * gm_w + ssd_w + conv_dim
        p = dict(
            g1=row2(norm1_g[layer]),
            wu=wl[:, :o1], wv=wl[:, o1:o2], wz=wl[:, o2:o3], wxbc=wl[:, o3:o4],
            wdt=_pad_lanes(wl[:, o4:]),
            lng=row2(gmlp_ln_g[layer]), lnb=row2(gmlp_ln_b[layer]),
            ws=gmlp_ws[layer], bst=_pad_lanes(gmlp_bs[layer].T),
            gog=row2(gmlp_out_g[layer]),
            convw=conv_w[layer], convb=row2(conv_b[layer]),
            dtb=_pad_lanes(row2(dt_bias[layer])), alog=_pad_lanes(row2(a_log[layer])),
            dskip=row2(jnp.repeat(d_skip[layer], SSD_HEAD_DIM)),
            sng=row2(ssd_norm_g[layer]),
            expand=expand, wout=w_out[layer].astype(BF16),
        )
        x1 = _mixer(x, p).reshape(b * l, d)
        ht, w, n, e2, r2 = _peer_prep(x1, row2(norm2_g[layer]), w_query[layer].astype(BF16),
                                   sub_keys1[layer].astype(BF16), sub_keys2[layer].astype(BF16))
        last = layer == depth - 1
        x = _peer_dense(ht, _pack_rows(expert_u[layer].astype(BF16)), _pack_rows(expert_v[layer].astype(BF16).T),
                        w, n, e2, r2, x1, row2(final_g), apply_final_norm=last).reshape(b, l, d)
    return x
```

```python
import functools
import math

import jax
import jax.numpy as jnp
from jax import lax
from jax.experimental import pallas as pl
from jax.experimental.pallas import tpu as pltpu

F32 = jnp.float32
BF16 = jnp.bfloat16
EPS = 1e-6

LANES = 128
VMEM_LIMIT_BYTES = 56 * 1024 * 1024

GM_HEADS = 8
SSD_HEADS = 16
SSD_HEAD_DIM = 64
SSD_GROUPS = 2
SSD_CONV = 4
PEER_HEADS = 8
PEER_TOPK = 16

MIX_TILE = 512
PREP_TILE = 512
DENSE_TILE = 1024
DENSE_EXPERT_BLOCK = 1024


def _dot(a, b):
    return jnp.dot(a, b, preferred_element_type=F32)


def _dot_nt(a, b):
    return lax.dot_general(a, b, (((1,), (1,)), ((), ())), preferred_element_type=F32)


def _split3(v):
    hi = v.astype(BF16)
    r = v - hi.astype(F32)
    mid = r.astype(BF16)
    lo = (r - mid.astype(F32)).astype(BF16)
    return hi, mid, lo


def _dot_f32_rhs(a_bf16, b):
    hi, mid, lo = _split3(b)
    return _dot(a_bf16, hi) + _dot(a_bf16, mid) + _dot(a_bf16, lo)


def _dot_f32_lhs(a, b_bf16):
    hi, mid, lo = _split3(a)
    return _dot(hi, b_bf16) + _dot(mid, b_bf16) + _dot(lo, b_bf16)


def _gelu(x):
    return 0.5 * x * (1.0 + lax.erf(x * math.sqrt(0.5)))


def _silu(x):
    return x * jax.nn.sigmoid(x)


def _rms(x, g):
    return x * lax.rsqrt(jnp.mean(x * x, axis=-1, keepdims=True) + EPS) * g


def _mixer_kernel(x_ref, g1_ref, wu_ref, wv_ref, wz_ref, wxbc_ref, wdt_ref,
                  lng_ref, lnb_ref, ws_ref, bst_ref, gog_ref,
                  convw_ref, convb_ref, dtb_ref, alog_ref, dskip_ref, sng_ref,
                  expand_ref, wout_ref,
                  o_ref,
                  st_ref, ext_ref, xbc_ref, dt_ref, da_ref, gm_ref, y_ref, mix_ref):
    tm = x_ref.shape[0]
    nchunk = tm // LANES
    gm_w = wu_ref.shape[1]
    ssd_w = wz_ref.shape[1]
    grp_w = ssd_w // SSD_GROUPS

    @pl.when(pl.program_id(1) == 0)
    def _():
        st_ref[...] = jnp.zeros_like(st_ref)
        ext_ref[0:8, :] = jnp.zeros((8, ext_ref.shape[1]), F32)

    x = x_ref[...]
    h = _rms(x, g1_ref[...]).astype(BF16)

    row = lax.broadcasted_iota(jnp.int32, (LANES, LANES), 0)
    col = lax.broadcasted_iota(jnp.int32, (LANES, LANES), 1)
    causal = row >= col

    u = _gelu(_dot(h, wu_ref[...]))
    vg = _gelu(_dot(h, wv_ref[...]))
    mu = jnp.mean(vg, axis=-1, keepdims=True)
    var = jnp.mean(jnp.square(vg - mu), axis=-1, keepdims=True)
    v = ((vg - mu) * lax.rsqrt(var + EPS) * lng_ref[...] + lnb_ref[...]).astype(BF16)
    for hd in range(GM_HEADS):
        wm = jnp.where(causal, ws_ref[hd], 0.0).astype(BF16)
        bcol = bst_ref[:, hd:hd + 1]
        for c in range(nchunk):
            rs = slice(c * LANES, (c + 1) * LANES)
            cs = slice(hd * LANES, (hd + 1) * LANES)
            gm_ref[rs, cs] = u[rs, cs] * (_dot(wm, v[rs, cs]) + bcol)
    mix_ref[:, 0:gm_w] = _rms(gm_ref[...], gog_ref[...]).astype(BF16)

    ext_ref[8:8 + tm, :] = _dot(h, wxbc_ref[...])
    conv = convb_ref[...]
    for k in range(SSD_CONV):
        conv = conv + ext_ref[8 - (SSD_CONV - 1) + k:8 - (SSD_CONV - 1) + k + tm, :] * convw_ref[k:k + 1, :]
    ext_ref[0:8, :] = ext_ref[tm:tm + 8, :]
    xbc_ref[...] = _silu(conv)

    dt = jax.nn.softplus(_dot(h, wdt_ref[...]) + dtb_ref[...])
    dt_ref[...] = dt
    da_ref[...] = dt * (-jnp.exp(alog_ref[...]))

    tri = causal.astype(BF16)
    lane = lax.broadcasted_iota(jnp.int32, (LANES, LANES), 1)
    low_half = lane < SSD_HEAD_DIM
    expand = expand_ref[...]
    dskip = dskip_ref[...]
    b_off = ssd_w
    c_off = ssd_w + SSD_GROUPS * LANES

    def chunk(c, carry):
        r0 = pl.multiple_of(c * LANES, LANES)
        rows = pl.ds(r0, LANES)
        acs = _dot_f32_rhs(tri, da_ref[rows, :])
        acs_t = acs.T
        last = acs[LANES - 1:LANES, :]
        dtx = _dot_f32_lhs(dt_ref[rows, :], expand)
        dsx = _dot_f32_lhs(jnp.exp(last - acs), expand)
        eax = _dot_f32_lhs(jnp.exp(acs), expand)
        cdx = _dot_f32_lhs(jnp.broadcast_to(jnp.exp(last), (8, LANES)), expand)[0:1, :]
        xs = xbc_ref[rows, 0:ssd_w]
        xdt = xs * dtx
        xw = xdt * dsx
        for g in range(SSD_GROUPS):
            gcols = slice(g * grp_w, (g + 1) * grp_w)
            bg = xbc_ref[rows, b_off + g * LANES:b_off + (g + 1) * LANES]
            cg = xbc_ref[rows, c_off + g * LANES:c_off + (g + 1) * LANES].astype(BF16)
            bt = bg.T.astype(BF16)
            cb = _dot(cg, bt)
            st = st_ref[:, gcols]
            yoff = _dot(cg, st.astype(BF16)) * eax[:, gcols]
            st_ref[:, gcols] = st * cdx[:, gcols] + _dot(bt, xw[:, gcols].astype(BF16))
            hpg = SSD_HEADS // SSD_GROUPS
            for kp in range(hpg // 2):
                h0 = g * hpg + 2 * kp
                pcols = slice(h0 * SSD_HEAD_DIM, h0 * SSD_HEAD_DIM + LANES)
                ms = []
                for hh in (h0, h0 + 1):
                    diff = acs[:, hh:hh + 1] - acs_t[hh:hh + 1, :]
                    ms.append(cb * jnp.exp(jnp.where(causal, diff, -jnp.inf)))
                lhs = jnp.concatenate(ms, axis=1).astype(BF16)
                xp = xdt[:, pcols]
                rhs = jnp.concatenate([jnp.where(low_half, xp, 0.0),
                                       jnp.where(low_half, 0.0, xp)], axis=0).astype(BF16)
                yd = _dot(lhs, rhs)
                y_ref[rows, pcols] = (yd + yoff[:, kp * LANES:(kp + 1) * LANES]
                                      + dskip[:, pcols] * xs[:, pcols])
        return carry

    lax.fori_loop(0, nchunk, chunk, 0)

    z = _dot(h, wz_ref[...])
    yg = y_ref[...] * _silu(z)
    for g in range(SSD_GROUPS):
        gcols = slice(g * grp_w, (g + 1) * grp_w)
        mix_ref[:, gm_w + g * grp_w:gm_w + (g + 1) * grp_w] = _rms(yg[:, gcols], sng_ref[:, gcols]).astype(BF16)

    o_ref[...] = x + _dot(mix_ref[...], wout_ref[...])


def _mixer(x, p):
    b, l, d = x.shape
    tm = min(MIX_TILE, l)
    conv_dim = p["wxbc"].shape[1]
    ssd_w = p["wz"].shape[1]
    gm_w = p["wu"].shape[1]
    full = lambda a: pl.BlockSpec(a.shape, lambda i, j: (0,) * a.ndim)
    names = ["g1", "wu", "wv", "wz", "wxbc", "wdt", "lng", "lnb", "ws", "bst", "gog",
             "convw", "convb", "dtb", "alog", "dskip", "sng", "expand", "wout"]
    args = [p[n] for n in names]
    return pl.pallas_call(
        _mixer_kernel,
        grid=(b, l // tm),
        in_specs=[pl.BlockSpec((None, tm, d), lambda i, j: (i, j, 0))] + [full(a) for a in args],
        out_specs=pl.BlockSpec((None, tm, d), lambda i, j: (i, j, 0)),
        out_shape=jax.ShapeDtypeStruct((b, l, d), F32),
        scratch_shapes=[
            pltpu.VMEM((LANES, ssd_w), F32),
            pltpu.VMEM((tm + 8, conv_dim), F32),
            pltpu.VMEM((tm, conv_dim), F32),
            pltpu.VMEM((tm, LANES), F32),
            pltpu.VMEM((tm, LANES), F32),
            pltpu.VMEM((tm, gm_w), F32),
            pltpu.VMEM((tm, ssd_w), F32),
            pltpu.VMEM((tm, gm_w + ssd_w), BF16),
        ],
        compiler_params=pltpu.CompilerParams(
            dimension_semantics=("arbitrary", "arbitrary"),
            vmem_limit_bytes=VMEM_LIMIT_BYTES),
        name="mixer",
    )(x, *args)


def _top16(s):
    rowi = lax.broadcasted_iota(jnp.int32, (PEER_TOPK, s.shape[1]), 0)
    out = jnp.zeros((PEER_TOPK, s.shape[1]), F32)
    rank = jnp.full(s.shape, float(PEER_TOPK), F32)
    for r in range(PEER_TOPK):
        m = jnp.max(s, axis=0, keepdims=True)
        out = jnp.where(rowi == r, m, out)
        hit = s == m
        rank = jnp.where(hit, float(r), rank)
        s = jnp.where(hit, -jnp.inf, s)
    return out, rank


def _candidates(a16, b16):
    parts = [a16 * b16[0:1]]
    for j in range(1, 8):
        parts.append(a16[0:8] * b16[j:j + 1])
    parts.append(a16[0:1] * b16[8:16])
    return jnp.concatenate(parts, axis=0)


def _prep_kernel(x_ref, g2_ref, wq_ref, k1_ref, k2_ref,
                 ht_ref, w_ref, n_ref, e2_ref, r2_ref,
                 q_ref):
    tt = x_ref.shape[0]
    h2 = _rms(x_ref[...], g2_ref[...])
    hb = h2.astype(BF16)
    ht_ref[...] = pltpu.bitcast(hb.astype(F32).T.astype(BF16), jnp.uint32)
    q_ref[...] = _dot(hb, wq_ref[...])
    k1 = k1_ref[...]
    k2 = k2_ref[...]
    dk = k1.shape[1]

    def group(lg, carry):
        rows = pl.ds(pl.multiple_of(lg * LANES, LANES), LANES)
        for hd in range(PEER_HEADS):
            q1 = q_ref[rows, hd * 2 * dk:hd * 2 * dk + dk].astype(BF16)
            q2 = q_ref[rows, hd * 2 * dk + dk:(hd + 1) * 2 * dk].astype(BF16)
            s1 = _dot_nt(k1, q1)
            s2 = _dot_nt(k2, q2)
            v1, _ = _top16(s1)
            v2, rank2 = _top16(s2)
            m1 = v1[0:1]
            m2 = v2[0:1]
            e1 = jnp.exp(s1 - m1)
            e2 = jnp.exp(s2 - m2)
            e1s = jnp.exp(v1 - m1)
            e2s = jnp.exp(v2 - m2)
            cand = _candidates(e1s, e2s)
            cw = cand
            zsum = jnp.zeros((1, LANES), F32)
            pm = zsum
            for _ in range(PEER_TOPK):
                pm = jnp.max(cw, axis=0, keepdims=True)
                zsum = zsum + pm
                cw = jnp.where(cw == pm, -1.0, cw)
            rz = 1.0 / zsum
            scaled = _candidates(e1s * rz, e2s)
            th = jnp.max(jnp.where(cand == pm, scaled, 0.0), axis=0, keepdims=True)
            w = e1 * rz
            ws0 = e1s[0:1] * rz
            cnt8 = jnp.zeros((1, LANES), F32)
            for j in range(8, PEER_TOPK):
                cnt8 = cnt8 + jnp.where(ws0 * e2s[j:j + 1] >= th, 1.0, 0.0)
            npair = jnp.where(w == ws0, cnt8, 0.0)
            for j in range(8):
                npair = npair + jnp.where(w * e2s[j:j + 1] >= th, 1.0, 0.0)
            w_ref[lg, hd] = pltpu.bitcast(w.astype(BF16), jnp.uint32)
            n_ref[lg, hd] = pltpu.bitcast(npair.astype(BF16), jnp.uint32)
            e2_ref[lg, hd] = pltpu.bitcast(e2.astype(BF16), jnp.uint32)
            r2_ref[lg, hd] = pltpu.bitcast(rank2.astype(BF16), jnp.uint32)
        return carry

    lax.fori_loop(0, tt // LANES, group, 0)


def _peer_prep(x1, g2, wq, k1, k2):
    t, d = x1.shape
    tt = min(PREP_TILE, t)
    ng = tt // LANES
    nk = k1.shape[0]
    full = lambda a: pl.BlockSpec(a.shape, lambda i: (0,) * a.ndim)
    return pl.pallas_call(
        _prep_kernel,
        grid=(t // tt,),
        in_specs=[pl.BlockSpec((tt, d), lambda i: (i, 0)), full(g2), full(wq), full(k1), full(k2)],
        out_specs=[
            pl.BlockSpec((d // 2, tt), lambda i: (0, i)),
            pl.BlockSpec((ng, PEER_HEADS, nk // 2, LANES), lambda i: (i, 0, 0, 0)),
            pl.BlockSpec((ng, PEER_HEADS, nk // 2, LANES), lambda i: (i, 0, 0, 0)),
            pl.BlockSpec((ng, PEER_HEADS, nk // 2, LANES), lambda i: (i, 0, 0, 0)),
            pl.BlockSpec((ng, PEER_HEADS, nk // 2, LANES), lambda i: (i, 0, 0, 0)),
        ],
        out_shape=[
            jax.ShapeDtypeStruct((d // 2, t), jnp.uint32),
            jax.ShapeDtypeStruct((t // LANES, PEER_HEADS, nk // 2, LANES), jnp.uint32),
            jax.ShapeDtypeStruct((t // LANES, PEER_HEADS, nk // 2, LANES), jnp.uint32),
            jax.ShapeDtypeStruct((t // LANES, PEER_HEADS, nk // 2, LANES), jnp.uint32),
            jax.ShapeDtypeStruct((t // LANES, PEER_HEADS, nk // 2, LANES), jnp.uint32),
        ],
        scratch_shapes=[pltpu.VMEM((tt, wq.shape[1]), F32)],
        compiler_params=pltpu.CompilerParams(
            dimension_semantics=("arbitrary",),
            vmem_limit_bytes=VMEM_LIMIT_BYTES),
        name="peer_prep",
    )(x1, g2, wq, k1, k2)


def _half_row(words, half):
    bits = (words << 16) if half == 0 else (words & jnp.uint32(0xFFFF0000))
    return lax.bitcast_convert_type(bits, F32).astype(BF16)


def _dense_kernel(ht_ref, u_ref, vt_ref, w_ref, n_ref, e2_ref, r2_ref, x_ref, fg_ref,
                  o_ref,
                  acc_ref, a_ref, b_ref, *, apply_final_norm):
    e = pl.program_id(1)
    eb = 2 * u_ref.shape[0]
    tt = ht_ref.shape[1]
    ib = eb // LANES

    @pl.when(e == 0)
    def _():
        acc_ref[...] = jnp.zeros_like(acc_ref)

    a_ref[...] = _dot(pltpu.bitcast(u_ref[...], BF16), pltpu.bitcast(ht_ref[...], BF16))

    def key_pair(s, carry):
        irow = pl.ds(e * (ib // 2) + s, 1)
        for lg in range(tt // LANES):
            lanes = slice(lg * LANES, (lg + 1) * LANES)
            gates = [jnp.zeros((LANES, LANES), BF16), jnp.zeros((LANES, LANES), BF16)]
            for hd in range(PEER_HEADS):
                e2v = pltpu.bitcast(e2_ref[lg, hd], BF16)
                r2v = pltpu.bitcast(r2_ref[lg, hd], BF16)
                ww = w_ref[lg, hd, irow, :]
                nn = n_ref[lg, hd, irow, :]
                for k in range(2):
                    val = e2v * _half_row(ww, k)
                    gates[k] = gates[k] + jnp.where(r2v < _half_row(nn, k), val, jnp.zeros_like(val))
            for k in range(2):
                rows = pl.ds(pl.multiple_of((2 * s + k) * LANES, LANES), LANES)
                b_ref[rows, lanes] = _gelu(a_ref[rows, lanes]).astype(BF16) * gates[k]
        return carry

    lax.fori_loop(0, ib // 2, key_pair, 0)

    acc_ref[...] += _dot(pltpu.bitcast(vt_ref[...], BF16), b_ref[...])

    @pl.when(e == pl.num_programs(1) - 1)
    def _():
        y = x_ref[...] + acc_ref[...].T
        if apply_final_norm:
            y = _rms(y, fg_ref[...])
        o_ref[...] = y


def _peer_dense(ht, u, vt, w, n, e2, r2, x1, fg, apply_final_norm):
    d, t = 2 * ht.shape[0], ht.shape[1]
    ne = 2 * u.shape[0]
    tt = min(DENSE_TILE, t)
    eb = min(DENSE_EXPERT_BLOCK, ne)
    ng = tt // LANES
    nk2 = w.shape[2]
    assert (eb // LANES) % 2 == 0
    return pl.pallas_call(
        functools.partial(_dense_kernel, apply_final_norm=apply_final_norm),
        grid=(t // tt, ne // eb),
        in_specs=[
            pl.BlockSpec((d // 2, tt), lambda i, j: (0, i)),
            pl.BlockSpec((eb // 2, d), lambda i, j: (j, 0)),
            pl.BlockSpec((d // 2, eb), lambda i, j: (0, j)),
            pl.BlockSpec((ng, PEER_HEADS, nk2, LANES), lambda i, j: (i, 0, 0, 0)),
            pl.BlockSpec((ng, PEER_HEADS, nk2, LANES), lambda i, j: (i, 0, 0, 0)),
            pl.BlockSpec((ng, PEER_HEADS, nk2, LANES), lambda i, j: (i, 0, 0, 0)),
            pl.BlockSpec((ng, PEER_HEADS, nk2, LANES), lambda i, j: (i, 0, 0, 0)),
            pl.BlockSpec((tt, d), lambda i, j: (i, 0), pipeline_mode=pl.Buffered(1)),
            pl.BlockSpec(fg.shape, lambda i, j: (0, 0)),
        ],
        out_specs=pl.BlockSpec((tt, d), lambda i, j: (i, 0), pipeline_mode=pl.Buffered(1)),
        out_shape=jax.ShapeDtypeStruct((t, d), F32),
        scratch_shapes=[
            pltpu.VMEM((d, tt), F32),
            pltpu.VMEM((eb, tt), F32),
            pltpu.VMEM((eb, tt), BF16),
        ],
        compiler_params=pltpu.CompilerParams(
            dimension_semantics=("arbitrary", "arbitrary"),
            vmem_limit_bytes=VMEM_LIMIT_BYTES),
        name="peer_dense",
    )(ht, u, vt, w, n, e2, r2, x1, fg)


def _pack_rows(a):
    h = lax.bitcast_convert_type(a, jnp.uint16).astype(jnp.uint32).reshape(a.shape[0] // 2, 2, a.shape[1])
    return h[:, 0] | (h[:, 1] << 16)


def _pad_lanes(a, width=LANES):
    return jnp.pad(a, [(0, 0)] * (a.ndim - 1) + [(0, width - a.shape[-1])])


def kernel(x, norm1_g, w_in, gmlp_ln_g, gmlp_ln_b, gmlp_ws, gmlp_bs, gmlp_out_g, conv_w, conv_b, dt_bias, a_log, d_skip, ssd_norm_g, w_out, norm2_g, w_query, sub_keys1, sub_keys2, expert_u, expert_v, final_g):
    b, l, d = x.shape
    depth = w_in.shape[0]
    gm_w = gmlp_ln_g.shape[1]
    ssd_w = ssd_norm_g.shape[1]
    conv_dim = conv_w.shape[2]
    assert ssd_w == SSD_HEADS * SSD_HEAD_DIM and gmlp_ws.shape[1:] == (GM_HEADS, LANES, LANES)
    assert l % LANES == 0 and (b * l) % LANES == 0
    expand = (jnp.arange(LANES)[:, None] == (jnp.arange(ssd_w)[None, :] // SSD_HEAD_DIM)).astype(BF16)
    row2 = lambda a: a.reshape(1, -1).astype(F32)
    for layer in range(depth):
        wl = w_in[layer].astype(BF16)
        o1, o2, o3, o4 = gm_w, 2 * gm_w, 2 * gm_w + ssd_w, 2 * gm_w + ssd_w + conv_dim
        p = dict(
            g1=row2(norm1_g[layer]),
            wu=wl[:, :o1], wv=wl[:, o1:o2], wz=wl[:, o2:o3], wxbc=wl[:, o3:o4],
            wdt=_pad_lanes(wl[:, o4:]),
            lng=row2(gmlp_ln_g[layer]), lnb=row2(gmlp_ln_b[layer]),
            ws=gmlp_ws[layer], bst=_pad_lanes(gmlp_bs[layer].T),
            gog=row2(gmlp_out_g[layer]),
            convw=conv_w[layer], convb=row2(conv_b[layer]),
            dtb=_pad_lanes(row2(dt_bias[layer])), alog=_pad_lanes(row2(a_log[layer])),
            dskip=row2(jnp.repeat(d_skip[layer], SSD_HEAD_DIM)),
            sng=row2(ssd_norm_g[layer]),
            expand=expand, wout=w_out[layer].astype(BF16),
        )
        x1 = _mixer(x, p).reshape(b * l, d)
        ht, w, n, e2, r2 = _peer_prep(x1, row2(norm2_g[layer]), w_query[layer].astype(BF16),
                                   sub_keys1[layer].astype(BF16), sub_keys2[layer].astype(BF16))
        last = layer == depth - 1
        x = _peer_dense(ht, _pack_rows(expert_u[layer].astype(BF16)), _pack_rows(expert_v[layer].astype(BF16).T),
                        w, n, e2, r2, x1, row2(final_g), apply_final_norm=last).reshape(b, l, d)
    return x
```

```python
import functools
import math

import jax
import jax.numpy as jnp
from jax import lax
from jax.experimental import pallas as pl
from jax.experimental.pallas import tpu as pltpu

F32 = jnp.float32
BF16 = jnp.bfloat16
EPS = 1e-6

LANES = 128
VMEM_LIMIT_BYTES = 56 * 1024 * 1024

GM_HEADS = 8
SSD_HEADS = 16
SSD_HEAD_DIM = 64
SSD_GROUPS = 2
SSD_CONV = 4
PEER_HEADS = 8
PEER_TOPK = 16

MIX_TILE = 512
PREP_TILE = 512
DENSE_TILE = 1024
DENSE_EXPERT_BLOCK = 512
DENSE_SUB = 2
DENSE_UNROLL = 1


def _dot(a, b):
    return jnp.dot(a, b, preferred_element_type=F32)


def _dot_nt(a, b):
    return lax.dot_general(a, b, (((1,), (1,)), ((), ())), preferred_element_type=F32)


def _split3(v):
    hi = v.astype(BF16)
    r = v - hi.astype(F32)
    mid = r.astype(BF16)
    lo = (r - mid.astype(F32)).astype(BF16)
    return hi, mid, lo


def _dot_f32_rhs(a_bf16, b):
    hi, mid, lo = _split3(b)
    return _dot(a_bf16, hi) + _dot(a_bf16, mid) + _dot(a_bf16, lo)


def _dot_f32_lhs(a, b_bf16):
    hi, mid, lo = _split3(a)
    return _dot(hi, b_bf16) + _dot(mid, b_bf16) + _dot(lo, b_bf16)


def _gelu(x):
    return 0.5 * x * (1.0 + lax.erf(x * math.sqrt(0.5)))


def _silu(x):
    return x * jax.nn.sigmoid(x)


def _rms(x, g):
    return x * lax.rsqrt(jnp.mean(x * x, axis=-1, keepdims=True) + EPS) * g


def _mixer_kernel(x_ref, g1_ref, wu_ref, wv_ref, wz_ref, wxbc_ref, wdt_ref,
                  lng_ref, lnb_ref, ws_ref, bst_ref, gog_ref,
                  convw_ref, convb_ref, dtb_ref, alog_ref, dskip_ref, sng_ref,
                  expand_ref, wout_ref,
                  o_ref,
                  st_ref, ext_ref, xbc_ref, dt_ref, da_ref, gm_ref, y_ref, mix_ref):
    tm = x_ref.shape[0]
    nchunk = tm // LANES
    gm_w = wu_ref.shape[1]
    ssd_w = wz_ref.shape[1]
    grp_w = ssd_w // SSD_GROUPS

    @pl.when(pl.program_id(1) == 0)
    def _():
        st_ref[...] = jnp.zeros_like(st_ref)
        ext_ref[0:8, :] = jnp.zeros((8, ext_ref.shape[1]), F32)

    x = x_ref[...]
    h = _rms(x, g1_ref[...]).astype(BF16)

    row = lax.broadcasted_iota(jnp.int32, (LANES, LANES), 0)
    col = lax.broadcasted_iota(jnp.int32, (LANES, LANES), 1)
    causal = row >= col

    u = _gelu(_dot(h, wu_ref[...]))
    vg = _gelu(_dot(h, wv_ref[...]))
    mu = jnp.mean(vg, axis=-1, keepdims=True)
    var = jnp.mean(jnp.square(vg - mu), axis=-1, keepdims=True)
    v = ((vg - mu) * lax.rsqrt(var + EPS) * lng_ref[...] + lnb_ref[...]).astype(BF16)
    for hd in range(GM_HEADS):
        wm = jnp.where(causal, ws_ref[hd], 0.0).astype(BF16)
        bcol = bst_ref[:, hd:hd + 1]
        for c in range(nchunk):
            rs = slice(c * LANES, (c + 1) * LANES)
            cs = slice(hd * LANES, (hd + 1) * LANES)
            gm_ref[rs, cs] = u[rs, cs] * (_dot(wm, v[rs, cs]) + bcol)
    mix_ref[:, 0:gm_w] = _rms(gm_ref[...], gog_ref[...]).astype(BF16)

    ext_ref[8:8 + tm, :] = _dot(h, wxbc_ref[...])
    conv = convb_ref[...]
    for k in range(SSD_CONV):
        conv = conv + ext_ref[8 - (SSD_CONV - 1) + k:8 - (SSD_CONV - 1) + k + tm, :] * convw_ref[k:k + 1, :]
    ext_ref[0:8, :] = ext_ref[tm:tm + 8, :]
    xbc_ref[...] = _silu(conv)

    dt = jax.nn.softplus(_dot(h, wdt_ref[...]) + dtb_ref[...])
    dt_ref[...] = dt
    da_ref[...] = dt * (-jnp.exp(alog_ref[...]))

    tri = causal.astype(BF16)
    lane = lax.broadcasted_iota(jnp.int32, (LANES, LANES), 1)
    low_half = lane < SSD_HEAD_DIM
    expand = expand_ref[...]
    dskip = dskip_ref[...]
    b_off = ssd_w
    c_off = ssd_w + SSD_GROUPS * LANES

    def chunk(c, carry):
        r0 = pl.multiple_of(c * LANES, LANES)
        rows = pl.ds(r0, LANES)
        acs = _dot_f32_rhs(tri, da_ref[rows, :])
        acs_t = acs.T
        last = acs[LANES - 1:LANES, :]
        dtx = _dot_f32_lhs(dt_ref[rows, :], expand)
        dsx = _dot_f32_lhs(jnp.exp(last - acs), expand)
        eax = _dot_f32_lhs(jnp.exp(acs), expand)
        cdx = _dot_f32_lhs(jnp.broadcast_to(jnp.exp(last), (8, LANES)), expand)[0:1, :]
        xs = xbc_ref[rows, 0:ssd_w]
        xdt = xs * dtx
        xw = xdt * dsx
        for g in range(SSD_GROUPS):
            gcols = slice(g * grp_w, (g + 1) * grp_w)
            bg = xbc_ref[rows, b_off + g * LANES:b_off + (g + 1) * LANES]
            cg = xbc_ref[rows, c_off + g * LANES:c_off + (g + 1) * LANES].astype(BF16)
            bt = bg.T.astype(BF16)
            cb = _dot(cg, bt)
            st = st_ref[:, gcols]
            yoff = _dot(cg, st.astype(BF16)) * eax[:, gcols]
            st_ref[:, gcols] = st * cdx[:, gcols] + _dot(bt, xw[:, gcols].astype(BF16))
            hpg = SSD_HEADS // SSD_GROUPS
            for kp in range(hpg // 2):
                h0 = g * hpg + 2 * kp
                pcols = slice(h0 * SSD_HEAD_DIM, h0 * SSD_HEAD_DIM + LANES)
                ms = []
                for hh in (h0, h0 + 1):
                    diff = acs[:, hh:hh + 1] - acs_t[hh:hh + 1, :]
                    ms.append(cb * jnp.exp(jnp.where(causal, diff, -jnp.inf)))
                lhs = jnp.concatenate(ms, axis=1).astype(BF16)
                xp = xdt[:, pcols]
                rhs = jnp.concatenate([jnp.where(low_half, xp, 0.0),
                                       jnp.where(low_half, 0.0, xp)], axis=0).astype(BF16)
                yd = _dot(lhs, rhs)
                y_ref[rows, pcols] = (yd + yoff[:, kp * LANES:(kp + 1) * LANES]
                                      + dskip[:, pcols] * xs[:, pcols])
        return carry

    lax.fori_loop(0, nchunk, chunk, 0)

    z = _dot(h, wz_ref[...])
    yg = y_ref[...] * _silu(z)
    for g in range(SSD_GROUPS):
        gcols = slice(g * grp_w, (g + 1) * grp_w)
        mix_ref[:, gm_w + g * grp_w:gm_w + (g + 1) * grp_w] = _rms(yg[:, gcols], sng_ref[:, gcols]).astype(BF16)

    o_ref[...] = x + _dot(mix_ref[...], wout_ref[...])


def _mixer(x, p):
    b, l, d = x.shape
    tm = min(MIX_TILE, l)
    conv_dim = p["wxbc"].shape[1]
    ssd_w = p["wz"].shape[1]
    gm_w = p["wu"].shape[1]
    full = lambda a: pl.BlockSpec(a.shape, lambda i, j: (0,) * a.ndim)
    names = ["g1", "wu", "wv", "wz", "wxbc", "wdt", "lng", "lnb", "ws", "bst", "gog",
             "convw", "convb", "dtb", "alog", "dskip", "sng", "expand", "wout"]
    args = [p[n] for n in names]
    return pl.pallas_call(
        _mixer_kernel,
        grid=(b, l // tm),
        in_specs=[pl.BlockSpec((None, tm, d), lambda i, j: (i, j, 0))] + [full(a) for a in args],
        out_specs=pl.BlockSpec((None, tm, d), lambda i, j: (i, j, 0)),
        out_shape=jax.ShapeDtypeStruct((b, l, d), F32),
        scratch_shapes=[
            pltpu.VMEM((LANES, ssd_w), F32),
            pltpu.VMEM((tm + 8, conv_dim), F32),
            pltpu.VMEM((tm, conv_dim), F32),
            pltpu.VMEM((tm, LANES), F32),
            pltpu.VMEM((tm, LANES), F32),
            pltpu.VMEM((tm, gm_w), F32),
            pltpu.VMEM((tm, ssd_w), F32),
            pltpu.VMEM((tm, gm_w + ssd_w), BF16),
        ],
        compiler_params=pltpu.CompilerParams(
            dimension_semantics=("arbitrary", "arbitrary"),
            vmem_limit_bytes=VMEM_LIMIT_BYTES),
        name="mixer",
    )(x, *args)


def _top16(s):
    rowi = lax.broadcasted_iota(jnp.int32, (PEER_TOPK, s.shape[1]), 0)
    out = jnp.zeros((PEER_TOPK, s.shape[1]), F32)
    rank = jnp.full(s.shape, float(PEER_TOPK), F32)
    for r in range(PEER_TOPK):
        m = jnp.max(s, axis=0, keepdims=True)
        out = jnp.where(rowi == r, m, out)
        hit = s == m
        rank = jnp.where(hit, float(r), rank)
        s = jnp.where(hit, -jnp.inf, s)
    return out, rank


def _candidates(a16, b16):
    parts = [a16 * b16[0:1]]
    for j in range(1, 8):
        parts.append(a16[0:8] * b16[j:j + 1])
    parts.append(a16[0:1] * b16[8:16])
    return jnp.concatenate(parts, axis=0)


def _prep_kernel(x_ref, g2_ref, wq_ref, k1_ref, k2_ref,
                 ht_ref, w_ref, n_ref, e2_ref, r2_ref,
                 q_ref):
    tt = x_ref.shape[0]
    h2 = _rms(x_ref[...], g2_ref[...])
    hb = h2.astype(BF16)
    ht_ref[...] = pltpu.bitcast(hb.astype(F32).T.astype(BF16), jnp.uint32)
    q_ref[...] = _dot(hb, wq_ref[...])
    k1 = k1_ref[...]
    k2 = k2_ref[...]
    dk = k1.shape[1]

    def group(lg, carry):
        rows = pl.ds(pl.multiple_of(lg * LANES, LANES), LANES)
        for hd in range(PEER_HEADS):
            q1 = q_ref[rows, hd * 2 * dk:hd * 2 * dk + dk].astype(BF16)
            q2 = q_ref[rows, hd * 2 * dk + dk:(hd + 1) * 2 * dk].astype(BF16)
            s1 = _dot_nt(k1, q1)
            s2 = _dot_nt(k2, q2)
            v1, _ = _top16(s1)
            v2, rank2 = _top16(s2)
            m1 = v1[0:1]
            m2 = v2[0:1]
            e1 = jnp.exp(s1 - m1)
            e2 = jnp.exp(s2 - m2)
            e1s = jnp.exp(v1 - m1)
            e2s = jnp.exp(v2 - m2)
            cand = _candidates(e1s, e2s)
            cw = cand
            zsum = jnp.zeros((1, LANES), F32)
            pm = zsum
            for _ in range(PEER_TOPK):
                pm = jnp.max(cw, axis=0, keepdims=True)
                zsum = zsum + pm
                cw = jnp.where(cw == pm, -1.0, cw)
            rz = 1.0 / zsum
            scaled = _candidates(e1s * rz, e2s)
            th = jnp.max(jnp.where(cand == pm, scaled, 0.0), axis=0, keepdims=True)
            w = e1 * rz
            ws0 = e1s[0:1] * rz
            cnt8 = jnp.zeros((1, LANES), F32)
            for j in range(8, PEER_TOPK):
                cnt8 = cnt8 + jnp.where(ws0 * e2s[j:j + 1] >= th, 1.0, 0.0)
            npair = jnp.where(w == ws0, cnt8, 0.0)
            for j in range(8):
                npair = npair + jnp.where(w * e2s[j:j + 1] >= th, 1.0, 0.0)
            w_ref[lg, hd] = pltpu.bitcast(w.astype(BF16), jnp.uint32)
            n_ref[lg, hd] = pltpu.bitcast(npair.astype(BF16), jnp.uint32)
            e2_ref[lg, hd] = pltpu.bitcast(e2.astype(BF16), jnp.uint32)
            r2_ref[lg, hd] = pltpu.bitcast(rank2.astype(BF16), jnp.uint32)
        return carry

    lax.fori_loop(0, tt // LANES, group, 0)


def _peer_prep(x1, g2, wq, k1, k2):
    t, d = x1.shape
    tt = min(PREP_TILE, t)
    ng = tt // LANES
    nk = k1.shape[0]
    full = lambda a: pl.BlockSpec(a.shape, lambda i: (0,) * a.ndim)
    return pl.pallas_call(
        _prep_kernel,
        grid=(t // tt,),
        in_specs=[pl.BlockSpec((tt, d), lambda i: (i, 0)), full(g2), full(wq), full(k1), full(k2)],
        out_specs=[
            pl.BlockSpec((d // 2, tt), lambda i: (0, i)),
            pl.BlockSpec((ng, PEER_HEADS, nk // 2, LANES), lambda i: (i, 0, 0, 0)),
            pl.BlockSpec((ng, PEER_HEADS, nk // 2, LANES), lambda i: (i, 0, 0, 0)),
            pl.BlockSpec((ng, PEER_HEADS, nk // 2, LANES), lambda i: (i, 0, 0, 0)),
            pl.BlockSpec((ng, PEER_HEADS, nk // 2, LANES), lambda i: (i, 0, 0, 0)),
        ],
        out_shape=[
            jax.ShapeDtypeStruct((d // 2, t), jnp.uint32),
            jax.ShapeDtypeStruct((t // LANES, PEER_HEADS, nk // 2, LANES), jnp.uint32),
            jax.ShapeDtypeStruct((t // LANES, PEER_HEADS, nk // 2, LANES), jnp.uint32),
            jax.ShapeDtypeStruct((t // LANES, PEER_HEADS, nk // 2, LANES), jnp.uint32),
            jax.ShapeDtypeStruct((t // LANES, PEER_HEADS, nk // 2, LANES), jnp.uint32),
        ],
        scratch_shapes=[pltpu.VMEM((tt, wq.shape[1]), F32)],
        compiler_params=pltpu.CompilerParams(
            dimension_semantics=("arbitrary",),
            vmem_limit_bytes=VMEM_LIMIT_BYTES),
        name="peer_prep",
    )(x1, g2, wq, k1, k2)


def _half_row(words, half):
    bits = (words << 16) if half == 0 else (words & jnp.uint32(0xFFFF0000))
    return lax.bitcast_convert_type(bits, F32).astype(BF16)


def _dense_kernel(ht_ref, u_ref, vt_ref, w_ref, n_ref, e2_ref, r2_ref, x_ref, fg_ref,
                  o_ref,
                  acc_ref, a0_ref, a1_ref, b0_ref, b1_ref, *, apply_final_norm, npairs):
    g = pl.program_id(1)
    eb = u_ref.shape[0]
    tt = ht_ref.shape[1]
    ib = eb // LANES
    hb = 2 * vt_ref.shape[0] // ib
    sr = DENSE_SUB * LANES
    nblocks = 2 * npairs

    @pl.when(g == 0)
    def _():
        acc_ref[...] = jnp.zeros_like(acc_ref)
        a0_ref[...] = jnp.zeros_like(a0_ref)
        a1_ref[...] = jnp.zeros_like(a1_ref)
        b0_ref[...] = jnp.zeros_like(b0_ref)
        b1_ref[...] = jnp.zeros_like(b1_ref)

    def phase(half, a_new, a_old, b_new, b_old):
        i0 = jnp.clip(2 * g + half - 1, 0, nblocks - 1) * ib

        def sub(s, carry):
            r0 = pl.multiple_of(s * sr, sr)
            urow = pl.ds(pl.multiple_of((half * eb + s * sr) // 2, sr // 2), sr // 2)
            a_new[pl.ds(r0, sr), :] = _dot(pltpu.bitcast(u_ref[urow, :], BF16),
                                           pltpu.bitcast(ht_ref[...], BF16))
            for lg in range(tt // LANES):
                lanes = slice(lg * LANES, (lg + 1) * LANES)
                gates = [jnp.zeros((LANES, LANES), BF16) for _ in range(DENSE_SUB)]
                for hd in range(PEER_HEADS):
                    e2v = pltpu.bitcast(e2_ref[lg, hd], BF16)
                    r2v = pltpu.bitcast(r2_ref[lg, hd], BF16)
                    for k in range(DENSE_SUB):
                        irow = pl.ds((i0 + s * DENSE_SUB) // 2 + k // 2, 1)
                        val = e2v * _half_row(w_ref[lg, hd, irow, :], k % 2)
                        nrow = _half_row(n_ref[lg, hd, irow, :], k % 2)
                        gates[k] = gates[k] + jnp.where(r2v < nrow, val, jnp.zeros_like(val))
                for k in range(DENSE_SUB):
                    rows = pl.ds(pl.multiple_of(r0 + k * LANES, LANES), LANES)
                    b_new[rows, lanes] = _gelu(a_old[rows, lanes]).astype(BF16) * gates[k]
            so = DENSE_SUB * hb
            orow = pl.ds(pl.multiple_of(s * so, so), so)
            vrow = pl.ds(pl.multiple_of(s * (so // 2), so // 2), so // 2)
            acc_ref[orow, :] += _dot(pltpu.bitcast(vt_ref[vrow, half * eb:(half + 1) * eb], BF16), b_old[...])
            return carry

        lax.fori_loop(0, ib // DENSE_SUB, sub, 0, unroll=DENSE_UNROLL)

    phase(0, a0_ref, a1_ref, b1_ref, b0_ref)
    phase(1, a1_ref, a0_ref, b0_ref, b1_ref)

    @pl.when(g == npairs)
    def _():
        y = x_ref[...] + acc_ref[...].T
        if apply_final_norm:
            y = _rms(y, fg_ref[...])
        o_ref[...] = y


def _peer_dense(ht, u, vt, w, n, e2, r2, x1, fg, apply_final_norm):
    d, t = 2 * ht.shape[0], ht.shape[1]
    ne = 2 * u.shape[0]
    tt = min(DENSE_TILE, t)
    eb = min(DENSE_EXPERT_BLOCK, ne // 2)
    npairs = ne // (2 * eb)
    ng = tt // LANES
    nk2 = w.shape[2]
    assert (eb // LANES) % 2 == 0 and DENSE_SUB % 2 == 0
    return pl.pallas_call(
        functools.partial(_dense_kernel, apply_final_norm=apply_final_norm, npairs=npairs),
        grid=(t // tt, npairs + 1),
        in_specs=[
            pl.BlockSpec((d // 2, tt), lambda i, g: (0, i)),
            pl.BlockSpec((eb, d), lambda i, g: (jnp.minimum(g, npairs - 1), 0)),
            pl.BlockSpec((d // 2, 2 * eb), lambda i, g: (0, jnp.maximum(g - 1, 0))),
            pl.BlockSpec((ng, PEER_HEADS, nk2, LANES), lambda i, g: (i, 0, 0, 0)),
            pl.BlockSpec((ng, PEER_HEADS, nk2, LANES), lambda i, g: (i, 0, 0, 0)),
            pl.BlockSpec((ng, PEER_HEADS, nk2, LANES), lambda i, g: (i, 0, 0, 0)),
            pl.BlockSpec((ng, PEER_HEADS, nk2, LANES), lambda i, g: (i, 0, 0, 0)),
            pl.BlockSpec((tt, d), lambda i, g: (i, 0), pipeline_mode=pl.Buffered(1)),
            pl.BlockSpec(fg.shape, lambda i, g: (0, 0)),
        ],
        out_specs=pl.BlockSpec((tt, d), lambda i, g: (i, 0), pipeline_mode=pl.Buffered(1)),
        out_shape=jax.ShapeDtypeStruct((t, d), F32),
        scratch_shapes=[
            pltpu.VMEM((d, tt), F32),
            pltpu.VMEM((eb, tt), F32),
            pltpu.VMEM((eb, tt), F32),
            pltpu.VMEM((eb, tt), BF16),
            pltpu.VMEM((eb, tt), BF16),
        ],
        compiler_params=pltpu.CompilerParams(
            dimension_semantics=("arbitrary", "arbitrary"),
            vmem_limit_bytes=VMEM_LIMIT_BYTES),
        name="peer_dense",
    )(ht, u, vt, w, n, e2, r2, x1, fg)


def _pack_rows(a):
    h = lax.bitcast_convert_type(a, jnp.uint16).astype(jnp.uint32).reshape(a.shape[0] // 2, 2, a.shape[1])
    return h[:, 0] | (h[:, 1] << 16)


def _pad_lanes(a, width=LANES):
    return jnp.pad(a, [(0, 0)] * (a.ndim - 1) + [(0, width - a.shape[-1])])


def kernel(x, norm1_g, w_in, gmlp_ln_g, gmlp_ln_b, gmlp_ws, gmlp_bs, gmlp_out_g, conv_w, conv_b, dt_bias, a_log, d_skip, ssd_norm_g, w_out, norm2_g, w_query, sub_keys1, sub_keys2, expert_u, expert_v, final_g):
    b, l, d = x.shape
    depth = w_in.shape[0]
    gm_w = gmlp_ln_g.shape[1]
    ssd_w = ssd_norm_g.shape[1]
    conv_dim = conv_w.shape[2]
    assert ssd_w == SSD_HEADS * SSD_HEAD_DIM and gmlp_ws.shape[1:] == (GM_HEADS, LANES, LANES)
    assert l % LANES == 0 and (b * l) % LANES == 0
    expand = (jnp.arange(LANES)[:, None] == (jnp.arange(ssd_w)[None, :] // SSD_HEAD_DIM)).astype(BF16)
    row2 = lambda a: a.reshape(1, -1).astype(F32)
    for layer in range(depth):
        wl = w_in[layer].astype(BF16)
        o1, o2, o3, o4 = gm_w, 2 * gm_w, 2 * gm_w + ssd_w, 2 * gm_w + ssd_w + conv_dim
        p = dict(
            g1=row2(norm1_g[layer]),
            wu=wl[:, :o1], wv=wl[:, o1:o2], wz=wl[:, o2:o3], wxbc=wl[:, o3:o4],
            wdt=_pad_lanes(wl[:, o4:]),
            lng=row2(gmlp_ln_g[layer]), lnb=row2(gmlp_ln_b[layer]),
            ws=gmlp_ws[layer], bst=_pad_lanes(gmlp_bs[layer].T),
            gog=row2(gmlp_out_g[layer]),
            convw=conv_w[layer], convb=row2(conv_b[layer]),
            dtb=_pad_lanes(row2(dt_bias[layer])), alog=_pad_lanes(row2(a_log[layer])),
            dskip=row2(jnp.repeat(d_skip[layer], SSD_HEAD_DIM)),
            sng=row2(ssd_norm_g[layer]),
            expand=expand, wout=w_out[layer].astype(BF16),
        )
        x1 = _mixer(x, p).reshape(b * l, d)
        ht, w, n, e2, r2 = _peer_prep(x1, row2(norm2_g[layer]), w_query[layer].astype(BF16),
                                   sub_keys1[layer].astype(BF16), sub_keys2[layer].astype(BF16))
        last = layer == depth - 1
        x = _peer_dense(ht, _pack_rows(expert_u[layer].astype(BF16)), _pack_rows(expert_v[layer].astype(BF16).T),
                        w, n, e2, r2, x1, row2(final_g), apply_final_norm=last).reshape(b, l, d)
    return x
```
